```python
import math
import jax, jax.numpy as jnp
from jax import lax
import numpy as np

D_MODEL = 1024
BATCH = 16
SEQ = 2048
DEPTH = 1

D_MIX = D_MODEL
GM_WIDTH = D_MIX // 2
GM_HEAD_DIM = 64
GM_HEADS = GM_WIDTH // GM_HEAD_DIM
GM_CHUNK = 128
SSM_WIDTH = D_MIX - GM_WIDTH
SSM_HEAD_DIM = 64
SSM_HEADS = SSM_WIDTH // SSM_HEAD_DIM
SSM_GROUPS = 2
SSM_STATE = 128
SSM_CONV = 4
SSM_CHUNK = 128
SSM_CONV_CH = SSM_WIDTH + 2 * SSM_GROUPS * SSM_STATE
D_FF = 4 * D_MODEL
EPS = 1e-6

IN_COLS = 2 * GM_WIDTH + SSM_WIDTH + SSM_CONV_CH + SSM_HEADS
SPLITS = (GM_WIDTH, 2 * GM_WIDTH, 2 * GM_WIDTH + SSM_WIDTH,
          2 * GM_WIDTH + SSM_WIDTH + SSM_CONV_CH)

kernel_name = "hybrid_gmlp_ssd_sandwich_block"


def rms_norm(x, w):
    xf = x.astype(jnp.float32)
    y = xf * lax.rsqrt(jnp.mean(xf * xf, axis=-1, keepdims=True) + EPS)
    return (y * w.astype(jnp.float32)).astype(x.dtype)


def layer_norm(x, w, b):
    xf = x.astype(jnp.float32)
    mu = jnp.mean(xf, axis=-1, keepdims=True)
    var = jnp.mean(jnp.square(xf - mu), axis=-1, keepdims=True)
    y = (xf - mu) * lax.rsqrt(var + EPS)
    return (y * w.astype(jnp.float32) + b.astype(jnp.float32)).astype(x.dtype)


def gmlp_mixer(u, v, ln_w, ln_b, w_s, b_s):
    bsz, L, _ = u.shape
    nc = L // GM_CHUNK
    u = jax.nn.gelu(u)
    v = jax.nn.gelu(v).reshape(bsz, L, GM_HEADS, GM_HEAD_DIM)
    v = layer_norm(v, ln_w, ln_b).reshape(bsz, nc, GM_CHUNK, GM_HEADS, GM_HEAD_DIM)
    causal = jnp.tril(jnp.ones((GM_CHUNK, GM_CHUNK), dtype=bool))
    w = jnp.where(causal[None], w_s, jnp.zeros((), w_s.dtype))
    mixed = jnp.einsum("hts,bcshp->bcthp", w, v) + b_s.T[None, None, :, :, None]
    return u * mixed.reshape(bsz, L, GM_WIDTH)


def causal_depthwise_conv(x, w, b):
    ch = x.shape[-1]
    y = lax.conv_general_dilated(
        x, w[:, None, :].astype(x.dtype), window_strides=(1,),
        padding=((SSM_CONV - 1, 0),), dimension_numbers=("NWC", "WIO", "NWC"),
        feature_group_count=ch)
    return y + b


def ssd_chunked(x, dt, a, bmat, cmat, d_skip):
    bsz, L = x.shape[0], x.shape[1]
    nc = L // SSM_CHUNK
    R = SSM_HEADS // SSM_GROUPS
    Q = SSM_CHUNK
    x = x.astype(jnp.float32).reshape(bsz, nc, Q, SSM_GROUPS, R, SSM_HEAD_DIM)
    dt = dt.astype(jnp.float32).reshape(bsz, nc, Q, SSM_GROUPS, R)
    bmat = bmat.astype(jnp.float32).reshape(bsz, nc, Q, SSM_GROUPS, SSM_STATE)
    cmat = cmat.astype(jnp.float32).reshape(bsz, nc, Q, SSM_GROUPS, SSM_STATE)
    a = a.astype(jnp.float32).reshape(SSM_GROUPS, R)
    d_skip = d_skip.astype(jnp.float32).reshape(SSM_GROUPS, R)

    a_cs = jnp.cumsum(dt * a, axis=2)
    x_dt = x * dt[..., None]

    causal = jnp.tril(jnp.ones((Q, Q), dtype=bool))[:, :, None, None]
    seg = a_cs[:, :, :, None] - a_cs[:, :, None, :]
    decay = jnp.exp(jnp.where(causal, seg, -jnp.inf))
    cb = jnp.einsum("bclgn,bcsgn->bclsg", cmat, bmat)
    y_diag = jnp.einsum("bclsg,bclsgr,bcsgrp->bclgrp", cb, decay, x_dt)

    decay_to_end = jnp.exp(a_cs[:, :, -1:] - a_cs)
    states = jnp.einsum("bcsgn,bcsgr,bcsgrp->bcgrpn", bmat, decay_to_end, x_dt)
    chunk_decay = jnp.exp(a_cs[:, :, -1])

    def step(h, inp):
        s, dcy = inp
        return h * dcy[..., None, None] + s, h
    h0 = jnp.zeros((bsz, SSM_GROUPS, R, SSM_HEAD_DIM, SSM_STATE), jnp.float32)
    _, prev = lax.scan(step, h0, (jnp.moveaxis(states, 1, 0), jnp.moveaxis(chunk_decay, 1, 0)))
    prev = jnp.moveaxis(prev, 0, 1)

    y_off = jnp.einsum("bclgn,bcgrpn,bclgr->bclgrp", cmat, prev, jnp.exp(a_cs))
    y = y_diag + y_off + d_skip[:, :, None] * x
    return y.reshape(bsz, L, SSM_HEADS * SSM_HEAD_DIM)


def mamba2_mixer(z, xbc, dt_raw, conv_w, conv_b, dt_bias, a_log, d_skip, norm_w):
    bsz, L, _ = z.shape
    xbc = jax.nn.silu(causal_depthwise_conv(xbc, conv_w, conv_b))
    xs = xbc[..., :SSM_WIDTH].reshape(bsz, L, SSM_HEADS, SSM_HEAD_DIM)
    bmat = xbc[..., SSM_WIDTH:SSM_WIDTH + SSM_GROUPS * SSM_STATE].reshape(bsz, L, SSM_GROUPS, SSM_STATE)
    cmat = xbc[..., SSM_WIDTH + SSM_GROUPS * SSM_STATE:].reshape(bsz, L, SSM_GROUPS, SSM_STATE)
    dt = jax.nn.softplus(dt_raw.astype(jnp.float32) + dt_bias.astype(jnp.float32))
    a = -jnp.exp(a_log.astype(jnp.float32))
    y = ssd_chunked(xs, dt, a, bmat, cmat, d_skip)
    y = y * jax.nn.silu(z.astype(jnp.float32))
    y = y.reshape(bsz, L, SSM_GROUPS, SSM_WIDTH // SSM_GROUPS)
    y = y * lax.rsqrt(jnp.mean(y * y, axis=-1, keepdims=True) + EPS)
    y = y.reshape(bsz, L, SSM_WIDTH) * norm_w.astype(jnp.float32)
    return y.astype(z.dtype)


def setup_inputs(seed: int = 0) -> dict:
    key = jax.random.key(seed)
    ks = jax.random.split(key, 24)
    f32 = jnp.float32

    def gain(k, shape):
        return 1.0 + 0.02 * jax.random.normal(k, shape, f32)

    x = jax.random.normal(ks[0], (BATCH, SEQ, D_MODEL), f32)
    norm_mix_pre = gain(ks[1], (DEPTH, D_MODEL))
    w_in = jax.random.normal(ks[2], (DEPTH, D_MODEL, IN_COLS), f32) * D_MODEL ** -0.5
    gm_ln_w = gain(ks[3], (DEPTH, GM_HEADS, GM_HEAD_DIM))
    gm_ln_b = 0.02 * jax.random.normal(ks[4], (DEPTH, GM_HEADS, GM_HEAD_DIM), f32)
    gm_w_s = jax.random.normal(ks[5], (DEPTH, GM_HEADS, GM_CHUNK, GM_CHUNK), f32) * GM_CHUNK ** -0.5
    gm_b_s = gain(ks[6], (DEPTH, GM_HEADS, GM_CHUNK))
    conv_w = jax.random.normal(ks[7], (DEPTH, SSM_CONV, SSM_CONV_CH), f32) * SSM_CONV ** -0.5
    conv_b = 0.02 * jax.random.normal(ks[8], (DEPTH, SSM_CONV_CH), f32)
    dt_min, dt_max = 1e-3, 1e-1
    u = jax.random.uniform(ks[9], (DEPTH, SSM_HEADS), f32)
    dt0 = jnp.maximum(jnp.exp(u * (math.log(dt_max) - math.log(dt_min)) + math.log(dt_min)), 1e-4)
    dt_bias = dt0 + jnp.log(-jnp.expm1(-dt0))
    a_log = jnp.log(jax.random.uniform(ks[10], (DEPTH, SSM_HEADS), f32, 1.0, 16.0))
    d_skip = gain(ks[11], (DEPTH, SSM_HEADS))
    ssm_norm_w = gain(ks[12], (DEPTH, SSM_WIDTH))
    w_out = jax.random.normal(ks[13], (DEPTH, D_MIX, D_MODEL), f32) * D_MIX ** -0.5
    norm_mix_post = gain(ks[14], (DEPTH, D_MODEL))
    norm_ffn_pre = gain(ks[15], (DEPTH, D_MODEL))
    w_up = jax.random.normal(ks[16], (DEPTH, D_MODEL, D_FF), f32) * D_MODEL ** -0.5
    w_down = jax.random.normal(ks[17], (DEPTH, D_FF, D_MODEL), f32) * D_FF ** -0.5
    norm_ffn_post = gain(ks[18], (DEPTH, D_MODEL))
    return {"x": x, "norm_mix_pre": norm_mix_pre, "w_in": w_in, "gm_ln_w": gm_ln_w,
            "gm_ln_b": gm_ln_b, "gm_w_s": gm_w_s, "gm_b_s": gm_b_s, "conv_w": conv_w,
            "conv_b": conv_b, "dt_bias": dt_bias, "a_log": a_log, "d_skip": d_skip,
            "ssm_norm_w": ssm_norm_w, "w_out": w_out, "norm_mix_post": norm_mix_post,
            "norm_ffn_pre": norm_ffn_pre, "w_up": w_up, "w_down": w_down,
            "norm_ffn_post": norm_ffn_post}


def reference(x, norm_mix_pre, w_in, gm_ln_w, gm_ln_b, gm_w_s, gm_b_s, conv_w, conv_b,
              dt_bias, a_log, d_skip, ssm_norm_w, w_out, norm_mix_post, norm_ffn_pre,
              w_up, w_down, norm_ffn_post):
    for i in range(DEPTH):
        h = rms_norm(x, norm_mix_pre[i])
        proj = jnp.einsum("bld,dk->blk", h, w_in[i])
        u_a, v_a, z_b, xbc_b, dt_b = jnp.split(proj, SPLITS, axis=-1)
        y_a = gmlp_mixer(u_a, v_a, gm_ln_w[i], gm_ln_b[i], gm_w_s[i], gm_b_s[i])
        y_b = mamba2_mixer(z_b, xbc_b, dt_b, conv_w[i], conv_b[i], dt_bias[i], a_log[i],
                           d_skip[i], ssm_norm_w[i])
        mix = jnp.concatenate([y_a, y_b], axis=-1)
        x = x + rms_norm(jnp.einsum("blk,kd->bld", mix, w_out[i]), norm_mix_post[i])
        h = rms_norm(x, norm_ffn_pre[i])
        f = jnp.square(jax.nn.relu(jnp.einsum("bld,df->blf", h, w_up[i])))
        x = x + rms_norm(jnp.einsum("blf,fd->bld", f, w_down[i]), norm_ffn_post[i])
    return x
```

```python
import functools
import math

import jax
import jax.numpy as jnp
from jax import lax
from jax.experimental import pallas as pl
from jax.experimental.pallas import tpu as pltpu

F32 = jnp.float32
BF16 = jnp.bfloat16

D_MODEL = 1024
GM_WIDTH = 512
HEAD_DIM = 64
HEADS = 8
CHUNK = 128
SSM_WIDTH = 512
GROUPS = 2
HEADS_PER_GROUP = HEADS // GROUPS
STATE = 128
CONV = 4
CONV_CH = SSM_WIDTH + 2 * GROUPS * STATE
D_FF = 4 * D_MODEL
EPS = 1e-6
MAIN_COLS = 2 * GM_WIDTH + SSM_WIDTH + CONV_CH
LANES = 128
PAD_ROWS = 8

VMEM_LIMIT_BYTES = 56 * 1024 * 1024

IN_TM = 1024
MIX_TC = 512
FFN_TM = 512
FF_BLOCK = 1024


def _rms(x, w):
    return x * lax.rsqrt(jnp.mean(x * x, axis=-1, keepdims=True) + EPS) * w


def _split_dot(a, b, passes):
    acc = None
    rem = a
    for _ in range(passes):
        piece = rem.astype(BF16)
        d = jnp.dot(piece, b, preferred_element_type=F32)
        acc = d if acc is None else acc + d
        rem = rem - piece.astype(F32)
    return acc


def _in_proj_kernel(x_ref, nw_ref, w_ref, wdt_ref, proj_ref, dt_ref):
    h = _rms(x_ref[...], nw_ref[...]).astype(BF16)
    for n0 in range(0, MAIN_COLS, 512):
        proj_ref[:, n0:n0 + 512] = jnp.dot(
            h, w_ref[:, n0:n0 + 512], preferred_element_type=F32).astype(BF16)
    dt_ref[...] = jnp.dot(h, wdt_ref[...], preferred_element_type=F32)


def _in_proj(x2d, nw, w_main, w_dt):
    t = x2d.shape[0]
    tm = min(IN_TM, t)
    const = lambda i: (0, 0)
    return pl.pallas_call(
        _in_proj_kernel,
        grid=(t // tm,),
        in_specs=[
            pl.BlockSpec((tm, D_MODEL), lambda i: (i, 0)),
            pl.BlockSpec((1, D_MODEL), const),
            pl.BlockSpec((D_MODEL, MAIN_COLS), const),
            pl.BlockSpec((D_MODEL, LANES), const),
        ],
        out_specs=[
            pl.BlockSpec((tm, MAIN_COLS), lambda i: (i, 0)),
            pl.BlockSpec((tm, LANES), lambda i: (i, 0)),
        ],
        out_shape=[
            jax.ShapeDtypeStruct((t, MAIN_COLS), BF16),
            jax.ShapeDtypeStruct((t, LANES), F32),
        ],
        compiler_params=pltpu.CompilerParams(
            dimension_semantics=("arbitrary",), vmem_limit_bytes=VMEM_LIMIT_BYTES),
        name="in_proj",
    )(x2d, nw, w_main, w_dt)


def _gelu(x):
    c = math.sqrt(2.0 / math.pi)
    return 0.5 * x * (1.0 + jnp.tanh(c * (x + 0.044715 * (x * x * x))))


def _silu(x):
    return x * (1.0 / (1.0 + jnp.exp(-x)))


def _softplus(x):
    return jnp.maximum(x, 0.0) + jnp.log1p(jnp.exp(-jnp.abs(x)))


def _mixer_kernel(proj_ref, dtraw_ref, lnw_ref, lnb_ref, wcat_ref, bs_ref, convw_ref,
                  convb_ref, dtb_ref, a_ref, dskip_ref, nrmw_ref, mavg_ref, tri_ref,
                  expand_ref, out_ref, xpad_ref, state_ref, vn_ref, xc_ref, dt_ref):
    tc = proj_ref.shape[0]
    n_chunks = tc // CHUNK
    j = pl.program_id(1)

    @pl.when(j == 0)
    def _():
        xpad_ref[0:PAD_ROWS, :] = jnp.zeros((PAD_ROWS, CONV_CH), F32)
        state_ref[...] = jnp.zeros(state_ref.shape, F32)

    v = _gelu(proj_ref[:, GM_WIDTH:2 * GM_WIDTH].astype(F32))
    mavg = mavg_ref[...]
    mu = jnp.dot(v.astype(BF16), mavg, preferred_element_type=F32)
    d = v - mu
    var = jnp.dot((d * d).astype(BF16), mavg, preferred_element_type=F32)
    vn_ref[...] = (d * lax.rsqrt(var + EPS) * lnw_ref[...] + lnb_ref[...]).astype(BF16)

    xpad_ref[PAD_ROWS:PAD_ROWS + tc, :] = proj_ref[:, MAIN_COLS - CONV_CH:MAIN_COLS].astype(F32)
    conv = convb_ref[...] + convw_ref[CONV - 1:CONV, :] * xpad_ref[PAD_ROWS:PAD_ROWS + tc, :]
    for k in range(1, CONV):
        conv = conv + convw_ref[CONV - 1 - k:CONV - k, :] * xpad_ref[PAD_ROWS - k:PAD_ROWS - k + tc, :]
    xpad_ref[0:PAD_ROWS, :] = xpad_ref[tc:tc + PAD_ROWS, :]
    xc_ref[...] = _silu(conv)

    dt_ref[...] = _softplus(dtraw_ref[...] + dtb_ref[...])

    lane = lax.broadcasted_iota(jnp.int32, (CHUNK, LANES), 1)
    row = lax.broadcasted_iota(jnp.int32, (CHUNK, LANES), 0)
    causal = row >= lane
    low_half = lane < HEAD_DIM

    def pair_rhs(tile):
        zero = jnp.zeros_like(tile)
        return jnp.concatenate(
            [jnp.where(low_half, tile, zero), jnp.where(low_half, zero, tile)], axis=0)

    def chunk_body(c, carry):
        r0 = pl.multiple_of(c * CHUNK, CHUNK)
        rows = pl.ds(r0, CHUNK)

        for k in range(GM_WIDTH // LANES):
            cols = slice(k * LANES, (k + 1) * LANES)
            mixed = jnp.dot(wcat_ref[k], pair_rhs(vn_ref[rows, cols]),
                            preferred_element_type=F32) + bs_ref[:, cols]
            u = _gelu(proj_ref[rows, cols].astype(F32))
            out_ref[rows, cols] = (u * mixed).astype(BF16)

        dt = dt_ref[rows, :]
        dta = dt * a_ref[...]
        a_cs = _split_dot_left(tri_ref[...], dta, 3)
        total = a_cs[CHUNK - 1:CHUNK, :]
        dte = jnp.exp(total - a_cs)
        ea = jnp.exp(a_cs)
        w_e = _split_dot(dt * dte, expand_ref[...], 2)
        ea_e = _split_dot(ea, expand_ref[...], 2)
        cd_e = ea_e[CHUNK - 1:CHUNK, :]
        packed = jnp.where(lane < HEADS, a_cs, pltpu.roll(dt, HEADS, 1))
        tr = packed.T

        xs = xc_ref[rows, 0:SSM_WIDTH]
        xs_b = xs.astype(BF16)
        xw_b = (xs * w_e).astype(BF16)
        y_parts = []
        for g in range(GROUPS):
            b_g = xc_ref[rows, SSM_WIDTH + g * STATE:SSM_WIDTH + (g + 1) * STATE]
            c_g = xc_ref[rows, SSM_WIDTH + (GROUPS + g) * STATE:
                         SSM_WIDTH + (GROUPS + g + 1) * STATE].astype(BF16)
            cb = lax.dot_general(c_g, b_g.astype(BF16), (((1,), (1,)), ((), ())),
                                 preferred_element_type=F32)
            gcols = slice(g * HEADS_PER_GROUP * HEAD_DIM, (g + 1) * HEADS_PER_GROUP * HEAD_DIM)
            state = state_ref[g]
            y_off = jnp.dot(c_g, state.astype(BF16), preferred_element_type=F32)
            scores = []
            for r in range(HEADS_PER_GROUP):
                h = g * HEADS_PER_GROUP + r
                seg = a_cs[:, h:h + 1] - tr[h:h + 1, :]
                dec = jnp.exp(jnp.where(causal, seg, -jnp.inf))
                scores.append((cb * dec * tr[HEADS + h:HEADS + h + 1, :]).astype(BF16))
            y_diag = []
            for p in range(HEADS_PER_GROUP // 2):
                lhs = jnp.concatenate([scores[2 * p], scores[2 * p + 1]], axis=1)
                tcol = g * 2 + p
                rhs = pair_rhs(xs_b[:, tcol * LANES:(tcol + 1) * LANES])
                y_diag.append(jnp.dot(lhs, rhs, preferred_element_type=F32))
            y_diag = jnp.concatenate(y_diag, axis=1)
            y_parts.append(y_diag + y_off * ea_e[:, gcols])
            upd = jnp.dot(b_g.T.astype(BF16), xw_b[:, gcols], preferred_element_type=F32)
            state_ref[g] = state * cd_e[:, gcols] + upd

        zs = proj_ref[rows, 2 * GM_WIDTH:2 * GM_WIDTH + SSM_WIDTH].astype(F32)
        y = (jnp.concatenate(y_parts, axis=1) + dskip_ref[...] * xs) * _silu(zs)
        half = SSM_WIDTH // GROUPS
        normed = []
        for g in range(GROUPS):
            yg = y[:, g * half:(g + 1) * half]
            normed.append(yg * lax.rsqrt(jnp.mean(yg * yg, axis=-1, keepdims=True) + EPS))
        out_ref[rows, GM_WIDTH:GM_WIDTH + SSM_WIDTH] = (
            jnp.concatenate(normed, axis=1) * nrmw_ref[...]).astype(BF16)
        return carry

    lax.fori_loop(0, n_chunks, chunk_body, 0)


def _split_dot_left(m, a, passes):
    acc = None
    rem = a
    for _ in range(passes):
        piece = rem.astype(BF16)
        d = jnp.dot(m, piece, preferred_element_type=F32)
        acc = d if acc is None else acc + d
        rem = rem - piece.astype(F32)
    return acc


def _mixer(proj, dt_raw, bsz, seq, lnw, lnb, wcat, bs_e, convw, convb, dtb, a_pad, dskip_e,
           nrmw, mavg, tri, expand):
    tc = min(MIX_TC, seq)
    nj = seq // tc
    blk = lambda b, j: (b * nj + j, 0)
    c2 = lambda b, j: (0, 0)
    c3 = lambda b, j: (0, 0, 0)
    return pl.pallas_call(
        _mixer_kernel,
        grid=(bsz, nj),
        in_specs=[
            pl.BlockSpec((tc, MAIN_COLS), blk),
            pl.BlockSpec((tc, LANES), blk),
            pl.BlockSpec((1, GM_WIDTH), c2),
            pl.BlockSpec((1, GM_WIDTH), c2),
            pl.BlockSpec((GM_WIDTH // LANES, CHUNK, 2 * CHUNK), c3),
            pl.BlockSpec((CHUNK, GM_WIDTH), c2),
            pl.BlockSpec((CONV, CONV_CH), c2),
            pl.BlockSpec((1, CONV_CH), c2),
            pl.BlockSpec((1, LANES), c2),
            pl.BlockSpec((1, LANES), c2),
            pl.BlockSpec((1, SSM_WIDTH), c2),
            pl.BlockSpec((1, SSM_WIDTH), c2),
            pl.BlockSpec((GM_WIDTH, GM_WIDTH), c2),
            pl.BlockSpec((CHUNK, CHUNK), c2),
            pl.BlockSpec((LANES, SSM_WIDTH), c2),
        ],
        out_specs=pl.BlockSpec((tc, GM_WIDTH + SSM_WIDTH), blk),
        out_shape=jax.ShapeDtypeStruct((bsz * seq, GM_WIDTH + SSM_WIDTH), BF16),
        scratch_shapes=[
            pltpu.VMEM((tc + PAD_ROWS, CONV_CH), F32),
            pltpu.VMEM((GROUPS, STATE, HEADS_PER_GROUP * HEAD_DIM), F32),
            pltpu.VMEM((tc, GM_WIDTH), BF16),
            pltpu.VMEM((tc, CONV_CH), F32),
            pltpu.VMEM((tc, LANES), F32),
        ],
        compiler_params=pltpu.CompilerParams(
            dimension_semantics=("arbitrary", "arbitrary"), vmem_limit_bytes=VMEM_LIMIT_BYTES),
        name="mixer",
    )(proj, dt_raw, lnw, lnb, wcat, bs_e, convw, convb, dtb, a_pad, dskip_e, nrmw, mavg, tri,
      expand)


def _out_ffn_kernel(x_ref, mix_ref, wout_ref, npost_ref, npre_ref, wup_ref, wdown_ref,
                    nffn_ref, out_ref):
    o = jnp.dot(mix_ref[...], wout_ref[...], preferred_element_type=F32)
    x1 = x_ref[...] + _rms(o, npost_ref[...])
    h = _rms(x1, npre_ref[...]).astype(BF16)
    acc = None
    for f0 in range(0, D_FF, FF_BLOCK):
        up = jnp.dot(h, wup_ref[:, f0:f0 + FF_BLOCK], preferred_element_type=F32)
        act = jnp.square(jnp.maximum(up, 0.0)).astype(BF16)
        dn = jnp.dot(act, wdown_ref[f0:f0 + FF_BLOCK, :], preferred_element_type=F32)
        acc = dn if acc is None else acc + dn
    out_ref[...] = x1 + _rms(acc, nffn_ref[...])


def _out_ffn(x2d, mix, wout, npost, npre, wup, wdown, nffn):
    t = x2d.shape[0]
    tm = min(FFN_TM, t)
    const = lambda i: (0, 0)
    resident = functools.partial(pl.BlockSpec, index_map=const, pipeline_mode=pl.Buffered(1))
    return pl.pallas_call(
        _out_ffn_kernel,
        grid=(t // tm,),
        in_specs=[
            pl.BlockSpec((tm, D_MODEL), lambda i: (i, 0)),
            pl.BlockSpec((tm, D_MODEL), lambda i: (i, 0)),
            resident((D_MODEL, D_MODEL)),
            pl.BlockSpec((1, D_MODEL), const),
            pl.BlockSpec((1, D_MODEL), const),
            resident((D_MODEL, D_FF)),
            resident((D_FF, D_MODEL)),
            pl.BlockSpec((1, D_MODEL), const),
        ],
        out_specs=pl.BlockSpec((tm, D_MODEL), lambda i: (i, 0)),
        out_shape=jax.ShapeDtypeStruct((t, D_MODEL), F32),
        compiler_params=pltpu.CompilerParams(
            dimension_semantics=("arbitrary",), vmem_limit_bytes=VMEM_LIMIT_BYTES),
        name="out_ffn",
    )(x2d, mix, wout, npost, npre, wup, wdown, nffn)


def _layer(x2d, bsz, seq, norm_mix_pre, w_in, gm_ln_w, gm_ln_b, gm_w_s, gm_b_s, conv_w, conv_b,
           dt_bias, a_log, d_skip, ssm_norm_w, w_out, norm_mix_post, norm_ffn_pre, w_up, w_down,
           norm_ffn_post):
    row = lambda p: p.reshape(1, -1).astype(F32)

    w_main = w_in[:, :MAIN_COLS].astype(BF16)
    w_dt = jnp.pad(w_in[:, MAIN_COLS:], ((0, 0), (0, LANES - HEADS))).astype(BF16)
    proj, dt_raw = _in_proj(x2d, row(norm_mix_pre), w_main, w_dt)

    causal = jnp.tril(jnp.ones((CHUNK, CHUNK), dtype=bool))
    w_s = jnp.where(causal[None], gm_w_s, 0.0).astype(BF16)
    wcat = w_s.reshape(HEADS // 2, 2, CHUNK, CHUNK).transpose(0, 2, 1, 3).reshape(
        HEADS // 2, CHUNK, 2 * CHUNK)
    bs_e = jnp.repeat(gm_b_s.T.astype(F32), HEAD_DIM, axis=1)
    dtb = jnp.pad(dt_bias.astype(F32), (0, LANES - HEADS)).reshape(1, LANES)
    a_pad = jnp.pad(-jnp.exp(a_log.astype(F32)), (0, LANES - HEADS)).reshape(1, LANES)
    dskip_e = jnp.repeat(d_skip.astype(F32), HEAD_DIM).reshape(1, SSM_WIDTH)
    head_of_col = jnp.arange(GM_WIDTH) // HEAD_DIM
    mavg = jnp.where(head_of_col[:, None] == head_of_col[None, :], 1.0 / HEAD_DIM, 0.0).astype(BF16)
    tri = causal.astype(BF16)
    expand = (jnp.arange(LANES)[:, None] == head_of_col[None, :]).astype(BF16)
    mix = _mixer(proj, dt_raw, bsz, seq, row(gm_ln_w), row(gm_ln_b), wcat, bs_e,
                 conv_w.astype(F32), row(conv_b), dtb, a_pad, dskip_e, row(ssm_norm_w),
                 mavg, tri, expand)

    return _out_ffn(x2d, mix, w_out.astype(BF16), row(norm_mix_post), row(norm_ffn_pre),
                    w_up.astype(BF16), w_down.astype(BF16), row(norm_ffn_post))


def kernel(x, norm_mix_pre, w_in, gm_ln_w, gm_ln_b, gm_w_s, gm_b_s, conv_w, conv_b, dt_bias, a_log, d_skip, ssm_norm_w, w_out, norm_mix_post, norm_ffn_pre, w_up, w_down, norm_ffn_post):
    bsz, seq, d = x.shape
    depth = w_in.shape[0]
    x2d = x.reshape(bsz * seq, d)
    for i in range(depth):
        x2d = _layer(x2d, bsz, seq, norm_mix_pre[i], w_in[i], gm_ln_w[i], gm_ln_b[i], gm_w_s[i],
                     gm_b_s[i], conv_w[i], conv_b[i], dt_bias[i], a_log[i], d_skip[i],
                     ssm_norm_w[i], w_out[i], norm_mix_post[i], norm_ffn_pre[i], w_up[i],
                     w_down[i], norm_ffn_post[i])
    return x2d.reshape(bsz, seq, d)
```

```python
import functools
import math

import jax
import jax.numpy as jnp
from jax import lax
from jax.experimental import pallas as pl
from jax.experimental.pallas import tpu as pltpu

F32 = jnp.float32
BF16 = jnp.bfloat16

D_MODEL = 1024
GM_WIDTH = 512
HEAD_DIM = 64
HEADS = 8
CHUNK = 128
SSM_WIDTH = 512
GROUPS = 2
HEADS_PER_GROUP = HEADS // GROUPS
GROUP_WIDTH = HEADS_PER_GROUP * HEAD_DIM
STATE = 128
CONV = 4
CONV_CH = SSM_WIDTH + 2 * GROUPS * STATE
D_FF = 4 * D_MODEL
EPS = 1e-6
MAIN_COLS = 2 * GM_WIDTH + SSM_WIDTH + CONV_CH
Z_COL = 2 * GM_WIDTH
XBC_COL = Z_COL + SSM_WIDTH
LANES = 128
PAD_ROWS = 8
LOG2E = 1.4426950408889634

VMEM_LIMIT_BYTES = 56 * 1024 * 1024

IN_TM = 1024
MIX_TC = 512
FFN_TM = 512
FF_BLOCK = 1024


def _rms(x, w):
    return x * lax.rsqrt(jnp.mean(x * x, axis=-1, keepdims=True) + EPS) * w


def _split_dot(a, b, passes):
    acc = None
    rem = a
    for _ in range(passes):
        piece = rem.astype(BF16)
        d = jnp.dot(piece, b, preferred_element_type=F32)
        acc = d if acc is None else acc + d
        rem = rem - piece.astype(F32)
    return acc


def _split_dot_left(m, a, passes):
    acc = None
    rem = a
    for _ in range(passes):
        piece = rem.astype(BF16)
        d = jnp.dot(m, piece, preferred_element_type=F32)
        acc = d if acc is None else acc + d
        rem = rem - piece.astype(F32)
    return acc


def _in_proj_kernel(x_ref, nw_ref, w_ref, wdt_ref, proj_ref, dt_ref):
    h = _rms(x_ref[...], nw_ref[...]).astype(BF16)
    for n0 in range(0, MAIN_COLS, 512):
        proj_ref[:, n0:n0 + 512] = jnp.dot(
            h, w_ref[:, n0:n0 + 512], preferred_element_type=F32).astype(BF16)
    dt_ref[...] = jnp.dot(h, wdt_ref[...], preferred_element_type=F32)


def _in_proj(x2d, nw, w_main, w_dt):
    t = x2d.shape[0]
    tm = min(IN_TM, t)
    const = lambda i: (0, 0)
    return pl.pallas_call(
        _in_proj_kernel,
        grid=(t // tm,),
        in_specs=[
            pl.BlockSpec((tm, D_MODEL), lambda i: (i, 0)),
            pl.BlockSpec((1, D_MODEL), const),
            pl.BlockSpec((D_MODEL, MAIN_COLS), const),
            pl.BlockSpec((D_MODEL, LANES), const),
        ],
        out_specs=[
            pl.BlockSpec((tm, MAIN_COLS), lambda i: (i, 0)),
            pl.BlockSpec((tm, LANES), lambda i: (i, 0)),
        ],
        out_shape=[
            jax.ShapeDtypeStruct((t, MAIN_COLS), BF16),
            jax.ShapeDtypeStruct((t, LANES), F32),
        ],
        compiler_params=pltpu.CompilerParams(
            dimension_semantics=("arbitrary",), vmem_limit_bytes=VMEM_LIMIT_BYTES),
        name="in_proj",
    )(x2d, nw, w_main, w_dt)


_GELU_A = -2.0 * math.sqrt(2.0 / math.pi) * LOG2E
_GELU_B = _GELU_A * 0.044715


def _gelu(x):
    return x / (1.0 + jnp.exp2(x * (_GELU_A + _GELU_B * (x * x))))


def _silu(x):
    return x / (1.0 + jnp.exp2(x * (-LOG2E)))


def _softplus(x):
    return jnp.maximum(x, 0.0) + jnp.log1p(jnp.exp(-jnp.abs(x)))


def _mixer_kernel(proj_ref, dtraw_ref, lnw_ref, lnb_ref, wcat_ref, bs_ref, convw_ref,
                  convb_ref, dtb_ref, a_ref, dskip_ref, nrmw_ref, mavg_ref, tri_ref,
                  expand_ref, out_ref, xpad_ref, state_ref, xc_ref):
    tc = proj_ref.shape[0]
    n_chunks = tc // CHUNK
    j = pl.program_id(1)

    @pl.when(j == 0)
    def _():
        xpad_ref[0:PAD_ROWS, :] = jnp.zeros((PAD_ROWS, CONV_CH), F32)
        state_ref[...] = jnp.zeros(state_ref.shape, F32)

    lane = lax.broadcasted_iota(jnp.int32, (CHUNK, LANES), 1)
    row = lax.broadcasted_iota(jnp.int32, (CHUNK, LANES), 0)
    causal = row >= lane
    low_half = lane < HEAD_DIM

    def pair_rhs(tile):
        return jnp.concatenate([jnp.where(low_half, tile, 0.0).astype(BF16),
                                jnp.where(low_half, 0.0, tile).astype(BF16)], axis=0)

    def chunk_body(c):
        r0 = c * CHUNK
        rows = slice(r0, r0 + CHUNK)
        prows = slice(r0 + PAD_ROWS, r0 + PAD_ROWS + CHUNK)

        v = _gelu(proj_ref[rows, GM_WIDTH:2 * GM_WIDTH].astype(F32))
        mu = jnp.dot(v.astype(BF16), mavg_ref[...], preferred_element_type=F32)
        d = v - mu
        var = jnp.dot((d * d).astype(BF16), mavg_ref[...], preferred_element_type=F32)
        vn = d * lax.rsqrt(var + EPS) * lnw_ref[...] + lnb_ref[...]
        for k in range(GM_WIDTH // LANES):
            cols = slice(k * LANES, (k + 1) * LANES)
            mixed = jnp.dot(wcat_ref[k], pair_rhs(vn[:, cols]),
                            preferred_element_type=F32) + bs_ref[:, cols]
            u = _gelu(proj_ref[rows, cols].astype(F32))
            out_ref[rows, cols] = (u * mixed).astype(BF16)

        xpad_ref[prows, :] = proj_ref[rows, XBC_COL:MAIN_COLS].astype(F32)
        for ct in range(CONV_CH // LANES):
            cols = slice(ct * LANES, (ct + 1) * LANES)
            acc = convb_ref[:, cols] + convw_ref[CONV - 1:CONV, cols] * xpad_ref[prows, cols]
            for k in range(1, CONV):
                acc = acc + convw_ref[CONV - 1 - k:CONV - k, cols] * xpad_ref[
                    r0 + PAD_ROWS - k:r0 + PAD_ROWS - k + CHUNK, cols]
            xc_ref[rows, cols] = _silu(acc)

        dt = _softplus(dtraw_ref[rows, :] + dtb_ref[...])
        dta = dt * (a_ref[...] * LOG2E)
        a_cs = _split_dot_left(tri_ref[...], dta, 3)
        total = a_cs[CHUNK - 1:CHUNK, :]
        dte = jnp.exp2(total - a_cs)
        ea = jnp.exp2(a_cs)
        w_e = _split_dot(dt * dte, expand_ref[...], 2)
        ea_e = _split_dot(ea, expand_ref[...], 2)
        cd_e = ea_e[CHUNK - 1:CHUNK, :]
        tr = (a_cs - jnp.log(dt) * LOG2E).T

        y_parts = []
        for g in range(GROUPS):
            gcols = slice(g * GROUP_WIDTH, (g + 1) * GROUP_WIDTH)
            b_g = xc_ref[rows, SSM_WIDTH + g * STATE:SSM_WIDTH + (g + 1) * STATE]
            c_g = xc_ref[rows, SSM_WIDTH + (GROUPS + g) * STATE:
                         SSM_WIDTH + (GROUPS + g + 1) * STATE].astype(BF16)
            cb = lax.dot_general(c_g, b_g.astype(BF16), (((1,), (1,)), ((), ())),
                                 preferred_element_type=F32)
            state = state_ref[g]
            y_off = jnp.dot(c_g, state.astype(BF16), preferred_element_type=F32)
            y_diag = []
            for p in range(HEADS_PER_GROUP // 2):
                scores = []
                for q in range(2):
                    h = g * HEADS_PER_GROUP + 2 * p + q
                    seg = a_cs[:, h:h + 1] - tr[h:h + 1, :]
                    scores.append((cb * jnp.exp2(jnp.where(causal, seg, -jnp.inf))).astype(BF16))
                tcol = g * (HEADS_PER_GROUP // 2) + p
                rhs = pair_rhs(xc_ref[rows, tcol * LANES:(tcol + 1) * LANES])
                y_diag.append(jnp.dot(jnp.concatenate(scores, axis=1), rhs,
                                      preferred_element_type=F32))
            y_parts.append(jnp.concatenate(y_diag, axis=1) + y_off * ea_e[:, gcols])
            xw = (xc_ref[rows, gcols] * w_e[:, gcols]).astype(BF16)
            upd = jnp.dot(b_g.T.astype(BF16), xw, preferred_element_type=F32)
            state_ref[g] = state * cd_e[:, gcols] + upd

        for g in range(GROUPS):
            gcols = slice(g * GROUP_WIDTH, (g + 1) * GROUP_WIDTH)
            zs = proj_ref[rows, Z_COL + g * GROUP_WIDTH:Z_COL + (g + 1) * GROUP_WIDTH].astype(F32)
            yg = (y_parts[g] + dskip_ref[:, gcols] * xc_ref[rows, gcols]) * _silu(zs)
            yn = yg * lax.rsqrt(jnp.mean(yg * yg, axis=-1, keepdims=True) + EPS)
            out_ref[rows, GM_WIDTH + g * GROUP_WIDTH:GM_WIDTH + (g + 1) * GROUP_WIDTH] = (
                yn * nrmw_ref[:, gcols]).astype(BF16)

    for c in range(n_chunks):
        chunk_body(c)
    xpad_ref[0:PAD_ROWS, :] = xpad_ref[tc:tc + PAD_ROWS, :]


def _mixer(proj, dt_raw, bsz, seq, lnw, lnb, wcat, bs_e, convw, convb, dtb, a_pad, dskip_e,
           nrmw, mavg, tri, expand):
    tc = min(MIX_TC, seq)
    nj = seq // tc
    blk = lambda b, j: (b * nj + j, 0)
    c2 = lambda b, j: (0, 0)
    c3 = lambda b, j: (0, 0, 0)
    return pl.pallas_call(
        _mixer_kernel,
        grid=(bsz, nj),
        in_specs=[
            pl.BlockSpec((tc, MAIN_COLS), blk),
            pl.BlockSpec((tc, LANES), blk),
            pl.BlockSpec((1, GM_WIDTH), c2),
            pl.BlockSpec((1, GM_WIDTH), c2),
            pl.BlockSpec((GM_WIDTH // LANES, CHUNK, 2 * CHUNK), c3),
            pl.BlockSpec((CHUNK, GM_WIDTH), c2),
            pl.BlockSpec((CONV, CONV_CH), c2),
            pl.BlockSpec((1, CONV_CH), c2),
            pl.BlockSpec((1, LANES), c2),
            pl.BlockSpec((1, LANES), c2),
            pl.BlockSpec((1, SSM_WIDTH), c2),
            pl.BlockSpec((1, SSM_WIDTH), c2),
            pl.BlockSpec((GM_WIDTH, GM_WIDTH), c2),
            pl.BlockSpec((CHUNK, CHUNK), c2),
            pl.BlockSpec((LANES, SSM_WIDTH), c2),
        ],
        out_specs=pl.BlockSpec((tc, GM_WIDTH + SSM_WIDTH), blk),
        out_shape=jax.ShapeDtypeStruct((bsz * seq, GM_WIDTH + SSM_WIDTH), BF16),
        scratch_shapes=[
            pltpu.VMEM((tc + PAD_ROWS, CONV_CH), F32),
            pltpu.VMEM((GROUPS, STATE, GROUP_WIDTH), F32),
            pltpu.VMEM((tc, CONV_CH), F32),
        ],
        compiler_params=pltpu.CompilerParams(
            dimension_semantics=("arbitrary", "arbitrary"), vmem_limit_bytes=VMEM_LIMIT_BYTES),
        name="mixer",
    )(proj, dt_raw, lnw, lnb, wcat, bs_e, convw, convb, dtb, a_pad, dskip_e, nrmw, mavg, tri,
      expand)


def _out_ffn_kernel(x_ref, mix_ref, wout_ref, npost_ref, npre_ref, wup_ref, wdown_ref,
                    nffn_ref, out_ref):
    o = jnp.dot(mix_ref[...], wout_ref[...], preferred_element_type=F32)
    x1 = x_ref[...] + _rms(o, npost_ref[...])
    h = _rms(x1, npre_ref[...]).astype(BF16)
    acc = None
    for f0 in range(0, D_FF, FF_BLOCK):
        up = jnp.dot(h, wup_ref[:, f0:f0 + FF_BLOCK], preferred_element_type=F32)
        act = jnp.square(jnp.maximum(up, 0.0)).astype(BF16)
        dn = jnp.dot(act, wdown_ref[f0:f0 + FF_BLOCK, :], preferred_element_type=F32)
        acc = dn if acc is None else acc + dn
    out_ref[...] = x1 + _rms(acc, nffn_ref[...])


def _out_ffn(x2d, mix, wout, npost, npre, wup, wdown, nffn):
    t = x2d.shape[0]
    tm = min(FFN_TM, t)
    const = lambda i: (0, 0)
    resident = functools.partial(pl.BlockSpec, index_map=const, pipeline_mode=pl.Buffered(1))
    return pl.pallas_call(
        _out_ffn_kernel,
        grid=(t // tm,),
        in_specs=[
            pl.BlockSpec((tm, D_MODEL), lambda i: (i, 0)),
            pl.BlockSpec((tm, D_MODEL), lambda i: (i, 0)),
            resident((D_MODEL, D_MODEL)),
            pl.BlockSpec((1, D_MODEL), const),
            pl.BlockSpec((1, D_MODEL), const),
            resident((D_MODEL, D_FF)),
            resident((D_FF, D_MODEL)),
            pl.BlockSpec((1, D_MODEL), const),
        ],
        out_specs=pl.BlockSpec((tm, D_MODEL), lambda i: (i, 0)),
        out_shape=jax.ShapeDtypeStruct((t, D_MODEL), F32),
        compiler_params=pltpu.CompilerParams(
            dimension_semantics=("arbitrary",), vmem_limit_bytes=VMEM_LIMIT_BYTES),
        name="out_ffn",
    )(x2d, mix, wout, npost, npre, wup, wdown, nffn)


def _layer(x2d, bsz, seq, norm_mix_pre, w_in, gm_ln_w, gm_ln_b, gm_w_s, gm_b_s, conv_w, conv_b,
           dt_bias, a_log, d_skip, ssm_norm_w, w_out, norm_mix_post, norm_ffn_pre, w_up, w_down,
           norm_ffn_post):
    row = lambda p: p.reshape(1, -1).astype(F32)

    w_main = w_in[:, :MAIN_COLS].astype(BF16)
    w_dt = jnp.pad(w_in[:, MAIN_COLS:], ((0, 0), (0, LANES - HEADS))).astype(BF16)
    proj, dt_raw = _in_proj(x2d, row(norm_mix_pre), w_main, w_dt)

    causal = jnp.tril(jnp.ones((CHUNK, CHUNK), dtype=bool))
    w_s = jnp.where(causal[None], gm_w_s, 0.0).astype(BF16)
    wcat = w_s.reshape(HEADS // 2, 2, CHUNK, CHUNK).transpose(0, 2, 1, 3).reshape(
        HEADS // 2, CHUNK, 2 * CHUNK)
    bs_e = jnp.repeat(gm_b_s.T.astype(F32), HEAD_DIM, axis=1)
    dtb = jnp.pad(dt_bias.astype(F32), (0, LANES - HEADS)).reshape(1, LANES)
    a_pad = jnp.pad(-jnp.exp(a_log.astype(F32)), (0, LANES - HEADS)).reshape(1, LANES)
    dskip_e = jnp.repeat(d_skip.astype(F32), HEAD_DIM).reshape(1, SSM_WIDTH)
    head_of_col = jnp.arange(GM_WIDTH) // HEAD_DIM
    mavg = jnp.where(head_of_col[:, None] == head_of_col[None, :], 1.0 / HEAD_DIM, 0.0).astype(BF16)
    tri = causal.astype(BF16)
    expand = (jnp.arange(LANES)[:, None] == head_of_col[None, :]).astype(BF16)
    mix = _mixer(proj, dt_raw, bsz, seq, row(gm_ln_w), row(gm_ln_b), wcat, bs_e,
                 conv_w.astype(F32), row(conv_b), dtb, a_pad, dskip_e, row(ssm_norm_w),
                 mavg, tri, expand)

    return _out_ffn(x2d, mix, w_out.astype(BF16), row(norm_mix_post), row(norm_ffn_pre),
                    w_up.astype(BF16), w_down.astype(BF16), row(norm_ffn_post))


def kernel(x, norm_mix_pre, w_in, gm_ln_w, gm_ln_b, gm_w_s, gm_b_s, conv_w, conv_b, dt_bias, a_log, d_skip, ssm_norm_w, w_out, norm_mix_post, norm_ffn_pre, w_up, w_down, norm_ffn_post):
    bsz, seq, d = x.shape
    depth = w_in.shape[0]
    x2d = x.reshape(bsz * seq, d)
    for i in range(depth):
        x2d = _layer(x2d, bsz, seq, norm_mix_pre[i], w_in[i], gm_ln_w[i], gm_ln_b[i], gm_w_s[i],
                     gm_b_s[i], conv_w[i], conv_b[i], dt_bias[i], a_log[i], d_skip[i],
                     ssm_norm_w[i], w_out[i], norm_mix_post[i], norm_ffn_pre[i], w_up[i],
                     w_down[i], norm_ffn_post[i])
    return x2d.reshape(bsz, seq, d)
```

```python
import functools
import math

import jax
import jax.numpy as jnp
from jax import lax
from jax.experimental import pallas as pl
from jax.experimental.pallas import tpu as pltpu

F32 = jnp.float32
BF16 = jnp.bfloat16

D_MODEL = 1024
GM_WIDTH = 512
HEAD_DIM = 64
HEADS = 8
CHUNK = 128
SSM_WIDTH = 512
GROUPS = 2
HEADS_PER_GROUP = HEADS // GROUPS
GROUP_WIDTH = HEADS_PER_GROUP * HEAD_DIM
STATE = 128
CONV = 4
CONV_CH = SSM_WIDTH + 2 * GROUPS * STATE
D_FF = 4 * D_MODEL
EPS = 1e-6
MAIN_COLS = 2 * GM_WIDTH + SSM_WIDTH + CONV_CH
Z_COL = 2 * GM_WIDTH
XBC_COL = Z_COL + SSM_WIDTH
MIX_WIDTH = GM_WIDTH + SSM_WIDTH
LANES = 128
PAD_ROWS = 8
LOG2E = 1.4426950408889634

VMEM_LIMIT_BYTES = 60 * 1024 * 1024

BLOCK = 512
PROJ_BLOCK = 512
FF_BLOCK = 1024


def _rms(x, w):
    return x * lax.rsqrt(jnp.mean(x * x, axis=-1, keepdims=True) + EPS) * w


def _split_dot(a, b, passes):
    acc = None
    rem = a
    for _ in range(passes):
        piece = rem.astype(BF16)
        d = jnp.dot(piece, b, preferred_element_type=F32)
        acc = d if acc is None else acc + d
        rem = rem - piece.astype(F32)
    return acc


def _split_dot_left(m, a, passes):
    acc = None
    rem = a
    for _ in range(passes):
        piece = rem.astype(BF16)
        d = jnp.dot(m, piece, preferred_element_type=F32)
        acc = d if acc is None else acc + d
        rem = rem - piece.astype(F32)
    return acc


_GELU_A = -2.0 * math.sqrt(2.0 / math.pi) * LOG2E
_GELU_B = _GELU_A * 0.044715


def _gelu(x):
    return x / (1.0 + jnp.exp2(x * (_GELU_A + _GELU_B * (x * x))))


def _silu(x):
    return x / (1.0 + jnp.exp2(x * (-LOG2E)))


def _softplus(x):
    return jnp.maximum(x, 0.0) + jnp.log1p(jnp.exp(-jnp.abs(x)))


def _layer_kernel(
        xa_ref, xb_ref, nmix_ref, win_ref, wdt_ref, lnw_ref, lnb_ref, wcat_ref, bs_ref, convw_ref,
        convb_ref, dtb_ref, a_ref, dskip_ref, nrmw_ref, mavg_ref, tri_ref, expand_ref, wout_ref,
        npost_ref, npre_ref, wup_ref, wdown_ref, nffn_ref,
        out_ref,
        proj_ref, dtraw_ref, mix_ref, xpad_ref, state_ref, xc_ref, ug_ref, vg_ref, zg_ref, dt_ref,
        *, blocks_per_seq, n_blocks):
    i = pl.program_id(0)
    n_chunks = BLOCK // CHUNK
    first_of_seq = lax.rem(jnp.minimum(i, n_blocks - 1), blocks_per_seq) == 0

    @pl.when(i == 0)
    def _():
        mix_ref[...] = jnp.zeros(mix_ref.shape, BF16)

    @pl.when(first_of_seq)
    def _():
        xpad_ref[0:PAD_ROWS, :] = jnp.zeros((PAD_ROWS, CONV_CH), F32)
        state_ref[...] = jnp.zeros(state_ref.shape, F32)

    o = jnp.dot(mix_ref[...], wout_ref[...], preferred_element_type=F32)
    x1 = xb_ref[...] + _rms(o, npost_ref[...])
    h2 = _rms(x1, npre_ref[...]).astype(BF16)

    h = _rms(xa_ref[...], nmix_ref[...]).astype(BF16)
    for n0 in range(0, MAIN_COLS, PROJ_BLOCK):
        proj_ref[:, n0:n0 + PROJ_BLOCK] = jnp.dot(
            h, win_ref[:, n0:n0 + PROJ_BLOCK], preferred_element_type=F32)
    dtraw_ref[...] = jnp.dot(h, wdt_ref[...], preferred_element_type=F32)

    lane = lax.broadcasted_iota(jnp.int32, (CHUNK, LANES), 1)
    row = lax.broadcasted_iota(jnp.int32, (CHUNK, LANES), 0)
    causal = row >= lane
    low_half = lane < HEAD_DIM

    def pair_rhs(tile):
        return jnp.concatenate([jnp.where(low_half, tile, 0.0).astype(BF16),
                                jnp.where(low_half, 0.0, tile).astype(BF16)], axis=0)

    def mixer_prep(c):
        r0 = c * CHUNK
        rows = slice(r0, r0 + CHUNK)
        prows = slice(r0 + PAD_ROWS, r0 + PAD_ROWS + CHUNK)
        ug_ref[rows, :] = _gelu(proj_ref[rows, 0:GM_WIDTH])
        vg_ref[rows, :] = _gelu(proj_ref[rows, GM_WIDTH:2 * GM_WIDTH])
        zg_ref[rows, :] = _silu(proj_ref[rows, Z_COL:Z_COL + SSM_WIDTH])
        dt_ref[rows, :] = _softplus(dtraw_ref[rows, :] + dtb_ref[...])
        xpad_ref[prows, :] = proj_ref[rows, XBC_COL:MAIN_COLS]
        for ct in range(CONV_CH // LANES):
            cols = slice(ct * LANES, (ct + 1) * LANES)
            acc = convb_ref[:, cols] + convw_ref[CONV - 1:CONV, cols] * xpad_ref[prows, cols]
            for k in range(1, CONV):
                acc = acc + convw_ref[CONV - 1 - k:CONV - k, cols] * xpad_ref[
                    r0 + PAD_ROWS - k:r0 + PAD_ROWS - k + CHUNK, cols]
            xc_ref[rows, cols] = _silu(acc)

    def mixer_core(c):
        r0 = c * CHUNK
        rows = slice(r0, r0 + CHUNK)

        v = vg_ref[rows, :]
        mu = jnp.dot(v.astype(BF16), mavg_ref[...], preferred_element_type=F32)
        d = v - mu
        var = jnp.dot((d * d).astype(BF16), mavg_ref[...], preferred_element_type=F32)
        vn = d * lax.rsqrt(var + EPS) * lnw_ref[...] + lnb_ref[...]
        for k in range(GM_WIDTH // LANES):
            cols = slice(k * LANES, (k + 1) * LANES)
            mixed = jnp.dot(wcat_ref[k], pair_rhs(vn[:, cols]),
                            preferred_element_type=F32) + bs_ref[:, cols]
            mix_ref[rows, cols] = (ug_ref[rows, cols] * mixed).astype(BF16)

        dt = dt_ref[rows, :]
        dta = dt * (a_ref[...] * LOG2E)
        a_cs = _split_dot_left(tri_ref[...], dta, 3)
        total = a_cs[CHUNK - 1:CHUNK, :]
        dte = jnp.exp2(total - a_cs)
        ea = jnp.exp2(a_cs)
        w_e = _split_dot(dt * dte, expand_ref[...], 2)
        ea_e = _split_dot(ea, expand_ref[...], 2)
        cd_e = ea_e[CHUNK - 1:CHUNK, :]
        tr = (a_cs - jnp.log(dt) * LOG2E).T

        y_parts = []
        for g in range(GROUPS):
            gcols = slice(g * GROUP_WIDTH, (g + 1) * GROUP_WIDTH)
            b_g = xc_ref[rows, SSM_WIDTH + g * STATE:SSM_WIDTH + (g + 1) * STATE]
            c_g = xc_ref[rows, SSM_WIDTH + (GROUPS + g) * STATE:
                         SSM_WIDTH + (GROUPS + g + 1) * STATE].astype(BF16)
            cb = lax.dot_general(c_g, b_g.astype(BF16), (((1,), (1,)), ((), ())),
                                 preferred_element_type=F32)
            state = state_ref[g]
            y_off = jnp.dot(c_g, state.astype(BF16), preferred_element_type=F32)
            y_diag = []
            for p in range(HEADS_PER_GROUP // 2):
                scores = []
                for q in range(2):
                    hd = g * HEADS_PER_GROUP + 2 * p + q
                    seg = a_cs[:, hd:hd + 1] - tr[hd:hd + 1, :]
                    scores.append((cb * jnp.exp2(jnp.where(causal, seg, -jnp.inf))).astype(BF16))
                tcol = g * (HEADS_PER_GROUP // 2) + p
                rhs = pair_rhs(xc_ref[rows, tcol * LANES:(tcol + 1) * LANES])
                y_diag.append(jnp.dot(jnp.concatenate(scores, axis=1), rhs,
                                      preferred_element_type=F32))
            y_parts.append(jnp.concatenate(y_diag, axis=1) + y_off * ea_e[:, gcols])
            xw = (xc_ref[rows, gcols] * w_e[:, gcols]).astype(BF16)
            upd = jnp.dot(b_g.T.astype(BF16), xw, preferred_element_type=F32)
            state_ref[g] = state * cd_e[:, gcols] + upd

        for g in range(GROUPS):
            gcols = slice(g * GROUP_WIDTH, (g + 1) * GROUP_WIDTH)
            yg = (y_parts[g] + dskip_ref[:, gcols] * xc_ref[rows, gcols]) * zg_ref[rows, gcols]
            yn = yg * lax.rsqrt(jnp.mean(yg * yg, axis=-1, keepdims=True) + EPS)
            mix_ref[rows, GM_WIDTH + g * GROUP_WIDTH:GM_WIDTH + (g + 1) * GROUP_WIDTH] = (
                yn * nrmw_ref[:, gcols]).astype(BF16)

    ff_blocks = list(range(0, D_FF, FF_BLOCK))
    acc = None
    for c in range(max(n_chunks, len(ff_blocks))):
        if c < n_chunks:
            mixer_prep(c)
            mixer_core(c)
        if c < len(ff_blocks):
            f0 = ff_blocks[c]
            up = jnp.dot(h2, wup_ref[:, f0:f0 + FF_BLOCK], preferred_element_type=F32)
            act = jnp.square(jnp.maximum(up, 0.0)).astype(BF16)
            dn = jnp.dot(act, wdown_ref[f0:f0 + FF_BLOCK, :], preferred_element_type=F32)
            acc = dn if acc is None else acc + dn
    xpad_ref[0:PAD_ROWS, :] = xpad_ref[BLOCK:BLOCK + PAD_ROWS, :]

    out_ref[...] = x1 + _rms(acc, nffn_ref[...])


def _layer(x2d, bsz, seq, norm_mix_pre, w_in, gm_ln_w, gm_ln_b, gm_w_s, gm_b_s, conv_w, conv_b,
           dt_bias, a_log, d_skip, ssm_norm_w, w_out, norm_mix_post, norm_ffn_pre, w_up, w_down,
           norm_ffn_post):
    t = bsz * seq
    assert seq % BLOCK == 0, (seq, BLOCK)
    n_blocks = t // BLOCK
    row = lambda p: p.reshape(1, -1).astype(F32)

    w_main = w_in[:, :MAIN_COLS].astype(BF16)
    w_dt = jnp.pad(w_in[:, MAIN_COLS:], ((0, 0), (0, LANES - HEADS))).astype(BF16)
    causal = jnp.tril(jnp.ones((CHUNK, CHUNK), dtype=bool))
    w_s = jnp.where(causal[None], gm_w_s, 0.0).astype(BF16)
    wcat = w_s.reshape(HEADS // 2, 2, CHUNK, CHUNK).transpose(0, 2, 1, 3).reshape(
        HEADS // 2, CHUNK, 2 * CHUNK)
    bs_e = jnp.repeat(gm_b_s.T.astype(F32), HEAD_DIM, axis=1)
    dtb = jnp.pad(dt_bias.astype(F32), (0, LANES - HEADS)).reshape(1, LANES)
    a_pad = jnp.pad(-jnp.exp(a_log.astype(F32)), (0, LANES - HEADS)).reshape(1, LANES)
    dskip_e = jnp.repeat(d_skip.astype(F32), HEAD_DIM).reshape(1, SSM_WIDTH)
    head_of_col = jnp.arange(GM_WIDTH) // HEAD_DIM
    mavg = jnp.where(head_of_col[:, None] == head_of_col[None, :], 1.0 / HEAD_DIM, 0.0).astype(BF16)
    tri = causal.astype(BF16)
    expand = (jnp.arange(LANES)[:, None] == head_of_col[None, :]).astype(BF16)

    def const(shape):
        zeros = (0,) * len(shape)
        return pl.BlockSpec(shape, lambda i: zeros, pipeline_mode=pl.Buffered(1))

    last = n_blocks - 1
    in_specs = [
        pl.BlockSpec((BLOCK, D_MODEL), lambda i: (jnp.minimum(i, last), 0)),
        pl.BlockSpec((BLOCK, D_MODEL), lambda i: (jnp.maximum(i - 1, 0), 0)),
        const((1, D_MODEL)),
        const((D_MODEL, MAIN_COLS)),
        const((D_MODEL, LANES)),
        const((1, GM_WIDTH)),
        const((1, GM_WIDTH)),
        const((HEADS // 2, CHUNK, 2 * CHUNK)),
        const((CHUNK, GM_WIDTH)),
        const((CONV, CONV_CH)),
        const((1, CONV_CH)),
        const((1, LANES)),
        const((1, LANES)),
        const((1, SSM_WIDTH)),
        const((1, SSM_WIDTH)),
        const((GM_WIDTH, GM_WIDTH)),
        const((CHUNK, CHUNK)),
        const((LANES, SSM_WIDTH)),
        const((MIX_WIDTH, D_MODEL)),
        const((1, D_MODEL)),
        const((1, D_MODEL)),
        const((D_MODEL, D_FF)),
        const((D_FF, D_MODEL)),
        const((1, D_MODEL)),
    ]
    return pl.pallas_call(
        functools.partial(_layer_kernel, blocks_per_seq=seq // BLOCK, n_blocks=n_blocks),
        grid=(n_blocks + 1,),
        in_specs=in_specs,
        out_specs=pl.BlockSpec((BLOCK, D_MODEL), lambda i: (jnp.maximum(i - 1, 0), 0)),
        out_shape=jax.ShapeDtypeStruct((t, D_MODEL), F32),
        scratch_shapes=[
            pltpu.VMEM((BLOCK, MAIN_COLS), F32),
            pltpu.VMEM((BLOCK, LANES), F32),
            pltpu.VMEM((BLOCK, MIX_WIDTH), BF16),
            pltpu.VMEM((BLOCK + PAD_ROWS, CONV_CH), F32),
            pltpu.VMEM((GROUPS, STATE, GROUP_WIDTH), F32),
            pltpu.VMEM((BLOCK, CONV_CH), F32),
            pltpu.VMEM((BLOCK, GM_WIDTH), F32),
            pltpu.VMEM((BLOCK, GM_WIDTH), F32),
            pltpu.VMEM((BLOCK, SSM_WIDTH), F32),
            pltpu.VMEM((BLOCK, LANES), F32),
        ],
        compiler_params=pltpu.CompilerParams(
            dimension_semantics=("arbitrary",), vmem_limit_bytes=VMEM_LIMIT_BYTES),
        name="hybrid_layer",
    )(x2d, x2d, row(norm_mix_pre), w_main, w_dt, row(gm_ln_w), row(gm_ln_b), wcat, bs_e,
      conv_w.astype(F32), row(conv_b), dtb, a_pad, dskip_e, row(ssm_norm_w), mavg, tri, expand,
      w_out.astype(BF16), row(norm_mix_post), row(norm_ffn_pre), w_up.astype(BF16),
      w_down.astype(BF16), row(norm_ffn_post))


def kernel(x, norm_mix_pre, w_in, gm_ln_w, gm_ln_b, gm_w_s, gm_b_s, conv_w, conv_b, dt_bias, a_log, d_skip, ssm_norm_w, w_out, norm_mix_post, norm_ffn_pre, w_up, w_down, norm_ffn_post):
    bsz, seq, d = x.shape
    depth = w_in.shape[0]
    x2d = x.reshape(bsz * seq, d)
    for i in range(depth):
        x2d = _layer(x2d, bsz, seq, norm_mix_pre[i], w_in[i], gm_ln_w[i], gm_ln_b[i], gm_w_s[i],
                     gm_b_s[i], conv_w[i], conv_b[i], dt_bias[i], a_log[i], d_skip[i],
                     ssm_norm_w[i], w_out[i], norm_mix_post[i], norm_ffn_pre[i], w_up[i],
                     w_down[i], norm_ffn_post[i])
    return x2d.reshape(bsz, seq, d)
```

```python
import functools
import math

import jax
import jax.numpy as jnp
from jax import lax
from jax.experimental import pallas as pl
from jax.experimental.pallas import tpu as pltpu

F32 = jnp.float32
BF16 = jnp.bfloat16

D_MODEL = 1024
GM_WIDTH = 512
HEAD_DIM = 64
HEADS = 8
CHUNK = 128
SSM_WIDTH = 512
GROUPS = 2
HEADS_PER_GROUP = HEADS // GROUPS
GROUP_WIDTH = HEADS_PER_GROUP * HEAD_DIM
STATE = 128
CONV = 4
CONV_CH = SSM_WIDTH + 2 * GROUPS * STATE
D_FF = 4 * D_MODEL
EPS = 1e-6
MAIN_COLS = 2 * GM_WIDTH + SSM_WIDTH + CONV_CH
Z_COL = 2 * GM_WIDTH
XBC_COL = Z_COL + SSM_WIDTH
MIX_WIDTH = GM_WIDTH + SSM_WIDTH
LANES = 128
PAD_ROWS = 8
LOG2E = 1.4426950408889634

VMEM_LIMIT_BYTES = 60 * 1024 * 1024

BLOCK = 512
PROJ_BLOCK = 512
FF_BLOCK = 1024


def _rms(x, w):
    return x * lax.rsqrt(jnp.mean(x * x, axis=-1, keepdims=True) + EPS) * w


def _split_dot(a, b, passes):
    acc = None
    rem = a
    for _ in range(passes):
        piece = rem.astype(BF16)
        d = jnp.dot(piece, b, preferred_element_type=F32)
        acc = d if acc is None else acc + d
        rem = rem - piece.astype(F32)
    return acc


def _split_dot_left(m, a, passes):
    acc = None
    rem = a
    for _ in range(passes):
        piece = rem.astype(BF16)
        d = jnp.dot(m, piece, preferred_element_type=F32)
        acc = d if acc is None else acc + d
        rem = rem - piece.astype(F32)
    return acc


_GELU_A = -2.0 * math.sqrt(2.0 / math.pi) * LOG2E
_GELU_B = _GELU_A * 0.044715


def _gelu(x):
    return x / (1.0 + jnp.exp2(x * (_GELU_A + _GELU_B * (x * x))))


def _silu(x):
    return x / (1.0 + jnp.exp2(x * (-LOG2E)))


def _softplus(x):
    return jnp.maximum(x, 0.0) + jnp.log1p(jnp.exp(-jnp.abs(x)))


def _layer_kernel(
        xa_ref, xb_ref, xn_ref, nmix_ref, win_ref, wdt_ref, lnw_ref, lnb_ref, wcat_ref, bs_ref,
        convw_ref, convb_ref, dtb_ref, a_ref, dskip_ref, nrmw_ref, mavg_ref, tri_ref, expand_ref,
        wout_ref, npost_ref, npre_ref, wup_ref, wdown_ref, nffn_ref,
        out_ref,
        proj_ref, dtraw_ref, mix_ref, xpad_ref, state_ref, xc_ref,
        *, blocks_per_seq, n_blocks):
    i = pl.program_id(0)
    n_chunks = BLOCK // CHUNK
    first_of_seq = lax.rem(jnp.minimum(i, n_blocks - 1), blocks_per_seq) == 0

    def in_proj_pieces(xv, dst):
        hn = _rms(xv, nmix_ref[...]).astype(BF16)

        def main(n0):
            proj_ref[dst, n0:n0 + PROJ_BLOCK] = jnp.dot(
                hn, win_ref[:, n0:n0 + PROJ_BLOCK], preferred_element_type=F32)

        def dt():
            dtraw_ref[dst, :] = jnp.dot(hn, wdt_ref[...], preferred_element_type=F32)

        return [functools.partial(main, n0) for n0 in range(0, MAIN_COLS, PROJ_BLOCK)] + [dt]

    @pl.when(i == 0)
    def _():
        mix_ref[...] = jnp.zeros(mix_ref.shape, BF16)
        for piece in in_proj_pieces(xa_ref[0:CHUNK, :], slice(0, CHUNK)):
            piece()

    @pl.when(first_of_seq)
    def _():
        xpad_ref[0:PAD_ROWS, :] = jnp.zeros((PAD_ROWS, CONV_CH), F32)
        state_ref[...] = jnp.zeros(state_ref.shape, F32)

    lane = lax.broadcasted_iota(jnp.int32, (CHUNK, LANES), 1)
    row = lax.broadcasted_iota(jnp.int32, (CHUNK, LANES), 0)
    causal = row >= lane
    low_half = lane < HEAD_DIM

    def pair_rhs(tile):
        return jnp.concatenate([jnp.where(low_half, tile, 0.0).astype(BF16),
                                jnp.where(low_half, 0.0, tile).astype(BF16)], axis=0)

    def mixers(r, fill):
        r0 = r * CHUNK
        rows = slice(r0, r0 + CHUNK)
        prows = slice(r0 + PAD_ROWS, r0 + PAD_ROWS + CHUNK)

        v = _gelu(proj_ref[rows, GM_WIDTH:2 * GM_WIDTH])
        mu = jnp.dot(v.astype(BF16), mavg_ref[...], preferred_element_type=F32)
        fill()
        d = v - mu
        var = jnp.dot((d * d).astype(BF16), mavg_ref[...], preferred_element_type=F32)
        fill()
        vn = d * lax.rsqrt(var + EPS) * lnw_ref[...] + lnb_ref[...]
        for k in range(GM_WIDTH // LANES):
            cols = slice(k * LANES, (k + 1) * LANES)
            mixed = jnp.dot(wcat_ref[k], pair_rhs(vn[:, cols]),
                            preferred_element_type=F32) + bs_ref[:, cols]
            mix_ref[rows, cols] = (_gelu(proj_ref[rows, cols]) * mixed).astype(BF16)
        fill()

        xpad_ref[prows, :] = proj_ref[rows, XBC_COL:MAIN_COLS]
        for ct in range(CONV_CH // LANES):
            cols = slice(ct * LANES, (ct + 1) * LANES)
            acc = convb_ref[:, cols] + convw_ref[CONV - 1:CONV, cols] * xpad_ref[prows, cols]
            for k in range(1, CONV):
                acc = acc + convw_ref[CONV - 1 - k:CONV - k, cols] * xpad_ref[
                    r0 + PAD_ROWS - k:r0 + PAD_ROWS - k + CHUNK, cols]
            xc_ref[rows, cols] = _silu(acc)

        dt = _softplus(dtraw_ref[rows, :] + dtb_ref[...])
        dta = dt * (a_ref[...] * LOG2E)
        a_cs = _split_dot_left(tri_ref[...], dta, 3)
        fill()
        total = a_cs[CHUNK - 1:CHUNK, :]
        dte = jnp.exp2(total - a_cs)
        ea = jnp.exp2(a_cs)
        w_e = _split_dot(dt * dte, expand_ref[...], 2)
        ea_e = _split_dot(ea, expand_ref[...], 2)
        fill()
        cd_e = ea_e[CHUNK - 1:CHUNK, :]
        tr = (a_cs - jnp.log(dt) * LOG2E).T

        y_parts = []
        for g in range(GROUPS):
            gcols = slice(g * GROUP_WIDTH, (g + 1) * GROUP_WIDTH)
            b_g = xc_ref[rows, SSM_WIDTH + g * STATE:SSM_WIDTH + (g + 1) * STATE]
            c_g = xc_ref[rows, SSM_WIDTH + (GROUPS + g) * STATE:
                         SSM_WIDTH + (GROUPS + g + 1) * STATE].astype(BF16)
            cb = lax.dot_general(c_g, b_g.astype(BF16), (((1,), (1,)), ((), ())),
                                 preferred_element_type=F32)
            state = state_ref[g]
            y_off = jnp.dot(c_g, state.astype(BF16), preferred_element_type=F32)
            fill()
            y_diag = []
            for p in range(HEADS_PER_GROUP // 2):
                scores = []
                for q in range(2):
                    hd = g * HEADS_PER_GROUP + 2 * p + q
                    seg = a_cs[:, hd:hd + 1] - tr[hd:hd + 1, :]
                    scores.append((cb * jnp.exp2(jnp.where(causal, seg, -jnp.inf))).astype(BF16))
                tcol = g * (HEADS_PER_GROUP // 2) + p
                rhs = pair_rhs(xc_ref[rows, tcol * LANES:(tcol + 1) * LANES])
                y_diag.append(jnp.dot(jnp.concatenate(scores, axis=1), rhs,
                                      preferred_element_type=F32))
            y_parts.append(jnp.concatenate(y_diag, axis=1) + y_off * ea_e[:, gcols])
            xw = (xc_ref[rows, gcols] * w_e[:, gcols]).astype(BF16)
            upd = jnp.dot(b_g.T.astype(BF16), xw, preferred_element_type=F32)
            state_ref[g] = state * cd_e[:, gcols] + upd
            fill()

        for g in range(GROUPS):
            gcols = slice(g * GROUP_WIDTH, (g + 1) * GROUP_WIDTH)
            zs = proj_ref[rows, Z_COL + g * GROUP_WIDTH:Z_COL + (g + 1) * GROUP_WIDTH]
            yg = (y_parts[g] + dskip_ref[:, gcols] * xc_ref[rows, gcols]) * _silu(zs)
            yn = yg * lax.rsqrt(jnp.mean(yg * yg, axis=-1, keepdims=True) + EPS)
            mix_ref[rows, GM_WIDTH + g * GROUP_WIDTH:GM_WIDTH + (g + 1) * GROUP_WIDTH] = (
                yn * nrmw_ref[:, gcols]).astype(BF16)

    half_rows = BLOCK // 2
    ff_blocks = list(range(0, D_FF, FF_BLOCK))
    cons = [dict() for _ in range(2)]

    def out_proj_piece(hf):
        hrows = slice(hf * half_rows, (hf + 1) * half_rows)
        o = jnp.dot(mix_ref[hrows, :], wout_ref[...], preferred_element_type=F32)
        x1 = xb_ref[hrows, :] + _rms(o, npost_ref[...])
        cons[hf]["x1"] = x1
        cons[hf]["h2"] = _rms(x1, npre_ref[...]).astype(BF16)

    def up_piece(hf, f0):
        up = jnp.dot(cons[hf]["h2"], wup_ref[:, f0:f0 + FF_BLOCK], preferred_element_type=F32)
        cons[hf][f0] = jnp.square(jnp.maximum(up, 0.0)).astype(BF16)

    def down_piece(hf, f0):
        dn = jnp.dot(cons[hf][f0], wdown_ref[f0:f0 + FF_BLOCK, :], preferred_element_type=F32)
        cons[hf]["acc"] = dn if "acc" not in cons[hf] else cons[hf]["acc"] + dn
        if f0 == ff_blocks[-1]:
            hrows = slice(hf * half_rows, (hf + 1) * half_rows)
            out_ref[hrows, :] = cons[hf]["x1"] + _rms(cons[hf]["acc"], nffn_ref[...])

    def both(*pieces):
        def run():
            for piece in pieces:
                piece()
        return run

    out_proj_piece(0)
    queue = [functools.partial(out_proj_piece, 1)]
    for hf in range(2):
        queue += [functools.partial(up_piece, hf, f0) for f0 in ff_blocks]
        queue += [functools.partial(down_piece, hf, f0) for f0 in ff_blocks]
    per_slot = [5, 4, 4, 4]
    assert sum(per_slot) == len(queue)

    for r in range(n_chunks):
        if r + 1 < n_chunks:
            nxt = in_proj_pieces(xa_ref[(r + 1) * CHUNK:(r + 2) * CHUNK, :],
                                 slice((r + 1) * CHUNK, (r + 2) * CHUNK))
        else:
            nxt = in_proj_pieces(xn_ref[...], slice(0, CHUNK))
        mine = [queue.pop(0) for _ in range(per_slot[r])]
        pairs = [both(nxt[0], nxt[1]), both(nxt[2], nxt[3]), both(nxt[4], nxt[5])]
        fillers = []
        while mine or pairs:
            if mine:
                fillers.append(mine.pop(0))
            if pairs:
                fillers.append(pairs.pop(0))

        def fill(fillers=fillers):
            if fillers:
                fillers.pop(0)()

        mixers(r, fill)
        while fillers:
            fill()

    xpad_ref[0:PAD_ROWS, :] = xpad_ref[BLOCK:BLOCK + PAD_ROWS, :]


def _layer(x2d, bsz, seq, norm_mix_pre, w_in, gm_ln_w, gm_ln_b, gm_w_s, gm_b_s, conv_w, conv_b,
           dt_bias, a_log, d_skip, ssm_norm_w, w_out, norm_mix_post, norm_ffn_pre, w_up, w_down,
           norm_ffn_post):
    t = bsz * seq
    assert seq % BLOCK == 0, (seq, BLOCK)
    n_blocks = t // BLOCK
    chunks_per_block = BLOCK // CHUNK
    row = lambda p: p.reshape(1, -1).astype(F32)

    w_main = w_in[:, :MAIN_COLS].astype(BF16)
    w_dt = jnp.pad(w_in[:, MAIN_COLS:], ((0, 0), (0, LANES - HEADS))).astype(BF16)
    causal = jnp.tril(jnp.ones((CHUNK, CHUNK), dtype=bool))
    w_s = jnp.where(causal[None], gm_w_s, 0.0).astype(BF16)
    wcat = w_s.reshape(HEADS // 2, 2, CHUNK, CHUNK).transpose(0, 2, 1, 3).reshape(
        HEADS // 2, CHUNK, 2 * CHUNK)
    bs_e = jnp.repeat(gm_b_s.T.astype(F32), HEAD_DIM, axis=1)
    dtb = jnp.pad(dt_bias.astype(F32), (0, LANES - HEADS)).reshape(1, LANES)
    a_pad = jnp.pad(-jnp.exp(a_log.astype(F32)), (0, LANES - HEADS)).reshape(1, LANES)
    dskip_e = jnp.repeat(d_skip.astype(F32), HEAD_DIM).reshape(1, SSM_WIDTH)
    head_of_col = jnp.arange(GM_WIDTH) // HEAD_DIM
    mavg = jnp.where(head_of_col[:, None] == head_of_col[None, :], 1.0 / HEAD_DIM, 0.0).astype(BF16)
    tri = causal.astype(BF16)
    expand = (jnp.arange(LANES)[:, None] == head_of_col[None, :]).astype(BF16)

    def const(shape):
        zeros = (0,) * len(shape)
        return pl.BlockSpec(shape, lambda i: zeros, pipeline_mode=pl.Buffered(1))

    last = n_blocks - 1
    last_chunk = n_blocks * chunks_per_block - 1
    in_specs = [
        pl.BlockSpec((BLOCK, D_MODEL), lambda i: (jnp.minimum(i, last), 0)),
        pl.BlockSpec((BLOCK, D_MODEL), lambda i: (jnp.maximum(i - 1, 0), 0)),
        pl.BlockSpec((CHUNK, D_MODEL),
                     lambda i: (jnp.minimum((i + 1) * chunks_per_block, last_chunk), 0)),
        const((1, D_MODEL)),
        const((D_MODEL, MAIN_COLS)),
        const((D_MODEL, LANES)),
        const((1, GM_WIDTH)),
        const((1, GM_WIDTH)),
        const((HEADS // 2, CHUNK, 2 * CHUNK)),
        const((CHUNK, GM_WIDTH)),
        const((CONV, CONV_CH)),
        const((1, CONV_CH)),
        const((1, LANES)),
        const((1, LANES)),
        const((1, SSM_WIDTH)),
        const((1, SSM_WIDTH)),
        const((GM_WIDTH, GM_WIDTH)),
        const((CHUNK, CHUNK)),
        const((LANES, SSM_WIDTH)),
        const((MIX_WIDTH, D_MODEL)),
        const((1, D_MODEL)),
        const((1, D_MODEL)),
        const((D_MODEL, D_FF)),
        const((D_FF, D_MODEL)),
        const((1, D_MODEL)),
    ]
    return pl.pallas_call(
        functools.partial(_layer_kernel, blocks_per_seq=seq // BLOCK, n_blocks=n_blocks),
        grid=(n_blocks + 1,),
        in_specs=in_specs,
        out_specs=pl.BlockSpec((BLOCK, D_MODEL), lambda i: (jnp.maximum(i - 1, 0), 0)),
        out_shape=jax.ShapeDtypeStruct((t, D_MODEL), F32),
        scratch_shapes=[
            pltpu.VMEM((BLOCK, MAIN_COLS), F32),
            pltpu.VMEM((BLOCK, LANES), F32),
            pltpu.VMEM((BLOCK, MIX_WIDTH), BF16),
            pltpu.VMEM((BLOCK + PAD_ROWS, CONV_CH), F32),
            pltpu.VMEM((GROUPS, STATE, GROUP_WIDTH), F32),
            pltpu.VMEM((BLOCK, CONV_CH), F32),
        ],
        compiler_params=pltpu.CompilerParams(
            dimension_semantics=("arbitrary",), vmem_limit_bytes=VMEM_LIMIT_BYTES),
        name="hybrid_layer",
    )(x2d, x2d, x2d, row(norm_mix_pre), w_main, w_dt, row(gm_ln_w), row(gm_ln_b), wcat, bs_e,
      conv_w.astype(F32), row(conv_b), dtb, a_pad, dskip_e, row(ssm_norm_w), mavg, tri, expand,
      w_out.astype(BF16), row(norm_mix_post), row(norm_ffn_pre), w_up.astype(BF16),
      w_down.astype(BF16), row(norm_ffn_post))


def kernel(x, norm_mix_pre, w_in, gm_ln_w, gm_ln_b, gm_w_s, gm_b_s, conv_w, conv_b, dt_bias, a_log, d_skip, ssm_norm_w, w_out, norm_mix_post, norm_ffn_pre, w_up, w_down, norm_ffn_post):
    bsz, seq, d = x.shape
    depth = w_in.shape[0]
    x2d = x.reshape(bsz * seq, d)
    for i in range(depth):
        x2d = _layer(x2d, bsz, seq, norm_mix_pre[i], w_in[i], gm_ln_w[i], gm_ln_b[i], gm_w_s[i],
                     gm_b_s[i], conv_w[i], conv_b[i], dt_bias[i], a_log[i], d_skip[i],
                     ssm_norm_w[i], w_out[i], norm_mix_post[i], norm_ffn_pre[i], w_up[i],
                     w_down[i], norm_ffn_post[i])
    return x2d.reshape(bsz, seq, d)
```

```python
import functools
import math

import jax
import jax.numpy as jnp
from jax import lax
from jax.experimental import pallas as pl
from jax.experimental.pallas import tpu as pltpu

F32 = jnp.float32
BF16 = jnp.bfloat16

D_MODEL = 1024
GM_WIDTH = 512
HEAD_DIM = 64
HEADS = 8
CHUNK = 128
SSM_WIDTH = 512
GROUPS = 2
HEADS_PER_GROUP = HEADS // GROUPS
GROUP_WIDTH = HEADS_PER_GROUP * HEAD_DIM
STATE = 128
CONV = 4
CONV_CH = SSM_WIDTH + 2 * GROUPS * STATE
D_FF = 4 * D_MODEL
EPS = 1e-6
MAIN_COLS = 2 * GM_WIDTH + SSM_WIDTH + CONV_CH
Z_COL = 2 * GM_WIDTH
XBC_COL = Z_COL + SSM_WIDTH
MIX_WIDTH = GM_WIDTH + SSM_WIDTH
LANES = 128
MEAN_WIDTH = 256
PAD_ROWS = 8
LOG2E = 1.4426950408889634

VMEM_LIMIT_BYTES = 60 * 1024 * 1024

BLOCK = 512
PROJ_BLOCK = 512
FF_BLOCK = 1024


def _rms(x, w):
    return x * lax.rsqrt(jnp.mean(x * x, axis=-1, keepdims=True) + EPS) * w


def _split_dot_left(m, a, passes):
    pieces = []
    rem = a
    for _ in range(passes):
        piece = rem.astype(BF16)
        pieces.append(piece)
        rem = rem - piece.astype(F32)
    d = jnp.dot(m, jnp.concatenate(pieces, axis=1), preferred_element_type=F32)
    width = a.shape[1]
    acc = d[:, 0:width]
    for k in range(1, passes):
        acc = acc + d[:, k * width:(k + 1) * width]
    return acc


_GELU_A = -2.0 * math.sqrt(2.0 / math.pi) * LOG2E
_GELU_B = _GELU_A * 0.044715


def _gelu(x):
    return x / (1.0 + jnp.exp2(x * (_GELU_A + _GELU_B * (x * x))))


def _silu(x):
    return x / (1.0 + jnp.exp2(x * (-LOG2E)))


def _softplus(x):
    return jnp.maximum(x, 0.0) + jnp.log1p(jnp.exp(-jnp.abs(x)))


def _layer_kernel(
        xa_ref, xb_ref, xn_ref, nmix_ref, win_ref, wdt_ref, lnw_ref, lnb_ref, wcat_ref, bs_ref,
        convw_ref, convb_ref, dtb_ref, a_ref, dskip_ref, nrmw_ref, mavg_ref, tri_ref,
        wout_ref, npost_ref, npre_ref, wup_ref, wdown_ref, nffn_ref,
        out_ref,
        proj_ref, dtraw_ref, mix_ref, xpad_ref, state_ref, xc_ref,
        *, blocks_per_seq, n_blocks):
    i = pl.program_id(0)
    n_chunks = BLOCK // CHUNK
    first_of_seq = lax.rem(jnp.minimum(i, n_blocks - 1), blocks_per_seq) == 0

    def in_proj_pieces(xv, dst):
        hn = _rms(xv, nmix_ref[...]).astype(BF16)

        def main(n0):
            proj_ref[dst, n0:n0 + PROJ_BLOCK] = jnp.dot(
                hn, win_ref[:, n0:n0 + PROJ_BLOCK], preferred_element_type=F32)

        def dt():
            dtraw_ref[dst, :] = jnp.dot(hn, wdt_ref[...], preferred_element_type=F32)

        return [functools.partial(main, n0) for n0 in range(0, MAIN_COLS, PROJ_BLOCK)] + [dt]

    @pl.when(i == 0)
    def _():
        mix_ref[...] = jnp.zeros(mix_ref.shape, BF16)
        for piece in in_proj_pieces(xa_ref[0:CHUNK, :], slice(0, CHUNK)):
            piece()

    @pl.when(first_of_seq)
    def _():
        xpad_ref[0:PAD_ROWS, :] = jnp.zeros((PAD_ROWS, CONV_CH), F32)
        state_ref[...] = jnp.zeros(state_ref.shape, F32)

    lane = lax.broadcasted_iota(jnp.int32, (CHUNK, LANES), 1)
    row = lax.broadcasted_iota(jnp.int32, (CHUNK, LANES), 0)
    causal = row >= lane
    low_half = lane < HEAD_DIM

    def pair_rhs(tile):
        return jnp.concatenate([jnp.where(low_half, tile, 0.0).astype(BF16),
                                jnp.where(low_half, 0.0, tile).astype(BF16)], axis=0)

    def mixers(r, fill):
        r0 = r * CHUNK
        rows = slice(r0, r0 + CHUNK)
        prows = slice(r0 + PAD_ROWS, r0 + PAD_ROWS + CHUNK)

        def head_mean(a):
            ab = a.astype(BF16)
            return jnp.concatenate(
                [jnp.dot(ab[:, c0:c0 + MEAN_WIDTH], mavg_ref[...], preferred_element_type=F32)
                 for c0 in range(0, GM_WIDTH, MEAN_WIDTH)], axis=1)

        v = _gelu(proj_ref[rows, GM_WIDTH:2 * GM_WIDTH])
        mu = head_mean(v)
        fill()
        d = v - mu
        var = head_mean(d * d)
        fill()
        vn = d * lax.rsqrt(var + EPS) * lnw_ref[...] + lnb_ref[...]
        for k in range(GM_WIDTH // LANES):
            cols = slice(k * LANES, (k + 1) * LANES)
            mixed = jnp.dot(wcat_ref[k], pair_rhs(vn[:, cols]),
                            preferred_element_type=F32) + bs_ref[:, cols]
            mix_ref[rows, cols] = (_gelu(proj_ref[rows, cols]) * mixed).astype(BF16)
        fill()

        xpad_ref[prows, :] = proj_ref[rows, XBC_COL:MAIN_COLS]
        for ct in range(CONV_CH // LANES):
            cols = slice(ct * LANES, (ct + 1) * LANES)
            acc = convb_ref[:, cols] + convw_ref[CONV - 1:CONV, cols] * xpad_ref[prows, cols]
            for k in range(1, CONV):
                acc = acc + convw_ref[CONV - 1 - k:CONV - k, cols] * xpad_ref[
                    r0 + PAD_ROWS - k:r0 + PAD_ROWS - k + CHUNK, cols]
            xc_ref[rows, cols] = _silu(acc)

        dt = _softplus(dtraw_ref[rows, :] + dtb_ref[...])
        dta = dt * (a_ref[...] * LOG2E)
        a_cs = _split_dot_left(tri_ref[...], dta, 3)
        fill()
        total = a_cs[CHUNK - 1:CHUNK, :]
        w_end = dt * jnp.exp2(total - a_cs)
        tr = jnp.where(lane < HEADS, a_cs - jnp.log(dt) * LOG2E, pltpu.roll(w_end, HEADS, 1)).T
        fill()

        y_parts = []
        for g in range(GROUPS):
            b_g = xc_ref[rows, SSM_WIDTH + g * STATE:SSM_WIDTH + (g + 1) * STATE]
            c_g = xc_ref[rows, SSM_WIDTH + (GROUPS + g) * STATE:
                         SSM_WIDTH + (GROUPS + g + 1) * STATE].astype(BF16)
            cb = lax.dot_general(c_g, b_g.astype(BF16), (((1,), (1,)), ((), ())),
                                 preferred_element_type=F32)
            state = state_ref[g]
            y_off = jnp.dot(c_g, state.astype(BF16), preferred_element_type=F32)
            fill()
            b_t = b_g.T
            y_tiles = []
            for p in range(HEADS_PER_GROUP // 2):
                pcols = slice(p * LANES, (p + 1) * LANES)
                scores, b_scaled, e_acs = [], [], []
                for q in range(2):
                    hd = g * HEADS_PER_GROUP + 2 * p + q
                    acs_b = jnp.broadcast_to(a_cs[:, hd:hd + 1], (CHUNK, LANES))
                    seg = acs_b - tr[hd:hd + 1, :]
                    scores.append((cb * jnp.exp2(jnp.where(causal, seg, -jnp.inf))).astype(BF16))
                    b_scaled.append((b_t * tr[HEADS + hd:HEADS + hd + 1, :]).astype(BF16))
                    e_acs.append(jnp.exp2(acs_b))
                tcol = g * (HEADS_PER_GROUP // 2) + p
                rhs = pair_rhs(xc_ref[rows, tcol * LANES:(tcol + 1) * LANES])
                lhs = jnp.concatenate([jnp.concatenate(scores, axis=1),
                                       jnp.concatenate(b_scaled, axis=1)], axis=0)
                res = jnp.dot(lhs, rhs, preferred_element_type=F32)
                decay = jnp.where(low_half, e_acs[0], e_acs[1])
                y_tiles.append(res[0:CHUNK, :] + y_off[:, pcols] * decay)
                state_ref[g, :, pcols] = (state[:, pcols] * decay[CHUNK - 1:CHUNK, :]
                                          + res[CHUNK:2 * CHUNK, :])
            y_parts.append(jnp.concatenate(y_tiles, axis=1))
            fill()

        for g in range(GROUPS):
            gcols = slice(g * GROUP_WIDTH, (g + 1) * GROUP_WIDTH)
            zs = proj_ref[rows, Z_COL + g * GROUP_WIDTH:Z_COL + (g + 1) * GROUP_WIDTH]
            yg = (y_parts[g] + dskip_ref[:, gcols] * xc_ref[rows, gcols]) * _silu(zs)
            yn = yg * lax.rsqrt(jnp.mean(yg * yg, axis=-1, keepdims=True) + EPS)
            mix_ref[rows, GM_WIDTH + g * GROUP_WIDTH:GM_WIDTH + (g + 1) * GROUP_WIDTH] = (
                yn * nrmw_ref[:, gcols]).astype(BF16)

    half_rows = BLOCK // 2
    ff_blocks = list(range(0, D_FF, FF_BLOCK))
    cons = [dict() for _ in range(2)]

    def out_proj_piece(hf):
        hrows = slice(hf * half_rows, (hf + 1) * half_rows)
        o = jnp.dot(mix_ref[hrows, :], wout_ref[...], preferred_element_type=F32)
        x1 = xb_ref[hrows, :] + _rms(o, npost_ref[...])
        cons[hf]["x1"] = x1
        cons[hf]["h2"] = _rms(x1, npre_ref[...]).astype(BF16)

    def up_piece(hf, f0):
        up = jnp.dot(cons[hf]["h2"], wup_ref[:, f0:f0 + FF_BLOCK], preferred_element_type=F32)
        cons[hf][f0] = jnp.square(jnp.maximum(up, 0.0)).astype(BF16)

    def down_piece(hf, f0):
        dn = jnp.dot(cons[hf][f0], wdown_ref[f0:f0 + FF_BLOCK, :], preferred_element_type=F32)
        cons[hf]["acc"] = dn if "acc" not in cons[hf] else cons[hf]["acc"] + dn
        if f0 == ff_blocks[-1]:
            hrows = slice(hf * half_rows, (hf + 1) * half_rows)
            out_ref[hrows, :] = cons[hf]["x1"] + _rms(cons[hf]["acc"], nffn_ref[...])

    def both(*pieces):
        def run():
            for piece in pieces:
                piece()
        return run

    out_proj_piece(0)
    queue = [functools.partial(out_proj_piece, 1)]
    for hf in range(2):
        queue += [functools.partial(up_piece, hf, f0) for f0 in ff_blocks]
        queue += [functools.partial(down_piece, hf, f0) for f0 in ff_blocks]
    per_slot = [5, 4, 4, 4]
    assert sum(per_slot) == len(queue)

    for r in range(n_chunks):
        if r + 1 < n_chunks:
            nxt = in_proj_pieces(xa_ref[(r + 1) * CHUNK:(r + 2) * CHUNK, :],
                                 slice((r + 1) * CHUNK, (r + 2) * CHUNK))
        else:
            nxt = in_proj_pieces(xn_ref[...], slice(0, CHUNK))
        mine = [queue.pop(0) for _ in range(per_slot[r])]
        pairs = [both(nxt[0], nxt[1]), both(nxt[2], nxt[3]), both(nxt[4], nxt[5])]
        fillers = []
        while mine or pairs:
            if mine:
                fillers.append(mine.pop(0))
            if pairs:
                fillers.append(pairs.pop(0))

        def fill(fillers=fillers):
            if fillers:
                fillers.pop(0)()

        mixers(r, fill)
        while fillers:
            fill()

    xpad_ref[0:PAD_ROWS, :] = xpad_ref[BLOCK:BLOCK + PAD_ROWS, :]


def _layer(x2d, bsz, seq, norm_mix_pre, w_in, gm_ln_w, gm_ln_b, gm_w_s, gm_b_s, conv_w, conv_b,
           dt_bias, a_log, d_skip, ssm_norm_w, w_out, norm_mix_post, norm_ffn_pre, w_up, w_down,
           norm_ffn_post):
    t = bsz * seq
    assert seq % BLOCK == 0, (seq, BLOCK)
    n_blocks = t // BLOCK
    chunks_per_block = BLOCK // CHUNK
    row = lambda p: p.reshape(1, -1).astype(F32)

    w_main = w_in[:, :MAIN_COLS].astype(BF16)
    w_dt = jnp.pad(w_in[:, MAIN_COLS:], ((0, 0), (0, LANES - HEADS))).astype(BF16)
    causal = jnp.tril(jnp.ones((CHUNK, CHUNK), dtype=bool))
    w_s = jnp.where(causal[None], gm_w_s, 0.0).astype(BF16)
    wcat = w_s.reshape(HEADS // 2, 2, CHUNK, CHUNK).transpose(0, 2, 1, 3).reshape(
        HEADS // 2, CHUNK, 2 * CHUNK)
    bs_e = jnp.repeat(gm_b_s.T.astype(F32), HEAD_DIM, axis=1)
    dtb = jnp.pad(dt_bias.astype(F32), (0, LANES - HEADS)).reshape(1, LANES)
    a_pad = jnp.pad(-jnp.exp(a_log.astype(F32)), (0, LANES - HEADS)).reshape(1, LANES)
    dskip_e = jnp.repeat(d_skip.astype(F32), HEAD_DIM).reshape(1, SSM_WIDTH)
    head_of_col = jnp.arange(MEAN_WIDTH) // HEAD_DIM
    mavg = jnp.where(head_of_col[:, None] == head_of_col[None, :], 1.0 / HEAD_DIM, 0.0).astype(BF16)
    tri = causal.astype(BF16)

    def const(shape):
        zeros = (0,) * len(shape)
        return pl.BlockSpec(shape, lambda i: zeros, pipeline_mode=pl.Buffered(1))

    last = n_blocks - 1
    last_chunk = n_blocks * chunks_per_block - 1
    in_specs = [
        pl.BlockSpec((BLOCK, D_MODEL), lambda i: (jnp.minimum(i, last), 0)),
        pl.BlockSpec((BLOCK, D_MODEL), lambda i: (jnp.maximum(i - 1, 0), 0)),
        pl.BlockSpec((CHUNK, D_MODEL),
                     lambda i: (jnp.minimum((i + 1) * chunks_per_block, last_chunk), 0)),
        const((1, D_MODEL)),
        const((D_MODEL, MAIN_COLS)),
        const((D_MODEL, LANES)),
        const((1, GM_WIDTH)),
        const((1, GM_WIDTH)),
        const((HEADS // 2, CHUNK, 2 * CHUNK)),
        const((CHUNK, GM_WIDTH)),
        const((CONV, CONV_CH)),
        const((1, CONV_CH)),
        const((1, LANES)),
        const((1, LANES)),
        const((1, SSM_WIDTH)),
        const((1, SSM_WIDTH)),
        const((MEAN_WIDTH, MEAN_WIDTH)),
        const((CHUNK, CHUNK)),
        const((MIX_WIDTH, D_MODEL)),
        const((1, D_MODEL)),
        const((1, D_MODEL)),
        const((D_MODEL, D_FF)),
        const((D_FF, D_MODEL)),
        const((1, D_MODEL)),
    ]
    return pl.pallas_call(
        functools.partial(_layer_kernel, blocks_per_seq=seq // BLOCK, n_blocks=n_blocks),
        grid=(n_blocks + 1,),
        in_specs=in_specs,
        out_specs=pl.BlockSpec((BLOCK, D_MODEL), lambda i: (jnp.maximum(i - 1, 0), 0)),
        out_shape=jax.ShapeDtypeStruct((t, D_MODEL), F32),
        scratch_shapes=[
            pltpu.VMEM((BLOCK, MAIN_COLS), F32),
            pltpu.VMEM((BLOCK, LANES), F32),
            pltpu.VMEM((BLOCK, MIX_WIDTH), BF16),
            pltpu.VMEM((BLOCK + PAD_ROWS, CONV_CH), F32),
            pltpu.VMEM((GROUPS, STATE, GROUP_WIDTH), F32),
            pltpu.VMEM((BLOCK, CONV_CH), F32),
        ],
        compiler_params=pltpu.CompilerParams(
            dimension_semantics=("arbitrary",), vmem_limit_bytes=VMEM_LIMIT_BYTES),
        name="hybrid_layer",
    )(x2d, x2d, x2d, row(norm_mix_pre), w_main, w_dt, row(gm_ln_w), row(gm_ln_b), wcat, bs_e,
      conv_w.astype(F32), row(conv_b), dtb, a_pad, dskip_e, row(ssm_norm_w), mavg, tri,
      w_out.astype(BF16), row(norm_mix_post), row(norm_ffn_pre), w_up.astype(BF16),
      w_down.astype(BF16), row(norm_ffn_post))


def kernel(x, norm_mix_pre, w_in, gm_ln_w, gm_ln_b, gm_w_s, gm_b_s, conv_w, conv_b, dt_bias, a_log, d_skip, ssm_norm_w, w_out, norm_mix_post, norm_ffn_pre, w_up, w_down, norm_ffn_post):
    bsz, seq, d = x.shape
    depth = w_in.shape[0]
    x2d = x.reshape(bsz * seq, d)
    for i in range(depth):
        x2d = _layer(x2d, bsz, seq, norm_mix_pre[i], w_in[i], gm_ln_w[i], gm_ln_b[i], gm_w_s[i],
                     gm_b_s[i], conv_w[i], conv_b[i], dt_bias[i], a_log[i], d_skip[i],
                     ssm_norm_w[i], w_out[i], norm_mix_post[i], norm_ffn_pre[i], w_up[i],
                     w_down[i], norm_ffn_post[i])
    return x2d.reshape(bsz, seq, d)
```

```python
import functools
import math

import jax
import jax.numpy as jnp
from jax import lax
from jax.experimental import pallas as pl
from jax.experimental.pallas import tpu as pltpu

F32 = jnp.float32
BF16 = jnp.bfloat16

D_MODEL = 1024
GM_WIDTH = 512
HEAD_DIM = 64
HEADS = 8
CHUNK = 128
SSM_WIDTH = 512
GROUPS = 2
HEADS_PER_GROUP = HEADS // GROUPS
GROUP_WIDTH = HEADS_PER_GROUP * HEAD_DIM
STATE = 128
CONV = 4
CONV_CH = SSM_WIDTH + 2 * GROUPS * STATE
D_FF = 4 * D_MODEL
EPS = 1e-6
MAIN_COLS = 2 * GM_WIDTH + SSM_WIDTH + CONV_CH
Z_COL = 2 * GM_WIDTH
XBC_COL = Z_COL + SSM_WIDTH
MIX_WIDTH = GM_WIDTH + SSM_WIDTH
LANES = 128
MEAN_WIDTH = 256
PAD_ROWS = 8
LOG2E = 1.4426950408889634

VMEM_LIMIT_BYTES = 60 * 1024 * 1024

BLOCK = 512
PROJ_BLOCK = 512
FF_BLOCK = 1024


def _rms(x, w):
    return x * lax.rsqrt(jnp.mean(x * x, axis=-1, keepdims=True) + EPS) * w


def _split_dot_left(m, a, passes):
    pieces = []
    rem = a
    for _ in range(passes):
        piece = rem.astype(BF16)
        pieces.append(piece)
        rem = rem - piece.astype(F32)
    d = jnp.dot(m, jnp.concatenate(pieces, axis=1), preferred_element_type=F32)
    width = a.shape[1]
    acc = d[:, 0:width]
    for k in range(1, passes):
        acc = acc + d[:, k * width:(k + 1) * width]
    return acc


_GELU_A = -2.0 * math.sqrt(2.0 / math.pi) * LOG2E
_GELU_B = _GELU_A * 0.044715


def _gelu(x):
    return x / (1.0 + jnp.exp2(x * (_GELU_A + _GELU_B * (x * x))))


def _silu(x):
    return x / (1.0 + jnp.exp2(x * (-LOG2E)))


def _softplus(x):
    return jnp.maximum(x, 0.0) + jnp.log1p(jnp.exp(-jnp.abs(x)))


def _layer_kernel(
        xa_ref, xb_ref, xn_ref, nmix_ref, win_ref, wdt_ref, lnw_ref, lnb_ref, wcat_ref, bs_ref,
        convw_ref, convb_ref, dtb_ref, a_ref, dskip_ref, nrmw_ref, mavg_ref, tri_ref,
        wout_ref, npost_ref, npre_ref, wup_ref, wdown_ref, nffn_ref,
        out_ref,
        proj_ref, dtraw_ref, mix_ref, xpad_ref, state_ref, xc_ref,
        *, blocks_per_seq, n_blocks):
    i = pl.program_id(0)
    n_chunks = BLOCK // CHUNK
    first_of_seq = lax.rem(jnp.minimum(i, n_blocks - 1), blocks_per_seq) == 0

    def in_proj_pieces(xv, dst):
        hn = _rms(xv, nmix_ref[...]).astype(BF16)

        def main(n0):
            proj_ref[dst, n0:n0 + PROJ_BLOCK] = jnp.dot(
                hn, win_ref[:, n0:n0 + PROJ_BLOCK], preferred_element_type=F32)

        def dt():
            dtraw_ref[dst, :] = jnp.dot(hn, wdt_ref[...], preferred_element_type=F32)

        return [functools.partial(main, n0) for n0 in range(0, MAIN_COLS, PROJ_BLOCK)] + [dt]

    @pl.when(i == 0)
    def _():
        mix_ref[...] = jnp.zeros(mix_ref.shape, BF16)
        for piece in in_proj_pieces(xa_ref[0:BLOCK // 2, :], slice(0, BLOCK // 2)):
            piece()

    @pl.when(first_of_seq)
    def _():
        xpad_ref[0:PAD_ROWS, :] = jnp.zeros((PAD_ROWS, CONV_CH), F32)
        state_ref[...] = jnp.zeros(state_ref.shape, F32)

    lane = lax.broadcasted_iota(jnp.int32, (CHUNK, LANES), 1)
    row = lax.broadcasted_iota(jnp.int32, (CHUNK, LANES), 0)
    causal = row >= lane
    low_half = lane < HEAD_DIM

    def pair_rhs(tile):
        return jnp.concatenate([jnp.where(low_half, tile, 0.0).astype(BF16),
                                jnp.where(low_half, 0.0, tile).astype(BF16)], axis=0)

    def gmlp(r):
        rows = slice(r * CHUNK, (r + 1) * CHUNK)

        def head_mean(a):
            ab = a.astype(BF16)
            return jnp.concatenate(
                [jnp.dot(ab[:, c0:c0 + MEAN_WIDTH], mavg_ref[...], preferred_element_type=F32)
                 for c0 in range(0, GM_WIDTH, MEAN_WIDTH)], axis=1)

        v = _gelu(proj_ref[rows, GM_WIDTH:2 * GM_WIDTH])
        mu = head_mean(v)
        yield
        d = v - mu
        var = head_mean(d * d)
        yield
        vn = d * lax.rsqrt(var + EPS) * lnw_ref[...] + lnb_ref[...]
        for k in range(GM_WIDTH // LANES):
            cols = slice(k * LANES, (k + 1) * LANES)
            mixed = jnp.dot(wcat_ref[k], pair_rhs(vn[:, cols]),
                            preferred_element_type=F32) + bs_ref[:, cols]
            mix_ref[rows, cols] = (_gelu(proj_ref[rows, cols]) * mixed).astype(BF16)
        yield

    def ssd(r):
        r0 = r * CHUNK
        rows = slice(r0, r0 + CHUNK)
        prows = slice(r0 + PAD_ROWS, r0 + PAD_ROWS + CHUNK)

        dt = _softplus(dtraw_ref[rows, :] + dtb_ref[...])
        dta = dt * (a_ref[...] * LOG2E)
        a_cs = _split_dot_left(tri_ref[...], dta, 3)
        yield

        xpad_ref[prows, :] = proj_ref[rows, XBC_COL:MAIN_COLS]
        for ct in range(CONV_CH // LANES):
            cols = slice(ct * LANES, (ct + 1) * LANES)
            acc = convb_ref[:, cols] + convw_ref[CONV - 1:CONV, cols] * xpad_ref[prows, cols]
            for k in range(1, CONV):
                acc = acc + convw_ref[CONV - 1 - k:CONV - k, cols] * xpad_ref[
                    r0 + PAD_ROWS - k:r0 + PAD_ROWS - k + CHUNK, cols]
            xc_ref[rows, cols] = _silu(acc)

        total = a_cs[CHUNK - 1:CHUNK, :]
        w_end = dt * jnp.exp2(total - a_cs)
        tr = jnp.where(lane < HEADS, a_cs - jnp.log(dt) * LOG2E, pltpu.roll(w_end, HEADS, 1)).T

        b_t, cb, y_off, state = [], [], [], []
        for g in range(GROUPS):
            b_g = xc_ref[rows, SSM_WIDTH + g * STATE:SSM_WIDTH + (g + 1) * STATE]
            c_g = xc_ref[rows, SSM_WIDTH + (GROUPS + g) * STATE:
                         SSM_WIDTH + (GROUPS + g + 1) * STATE].astype(BF16)
            cb.append(lax.dot_general(c_g, b_g.astype(BF16), (((1,), (1,)), ((), ())),
                                      preferred_element_type=F32))
            state.append(state_ref[g])
            y_off.append(jnp.dot(c_g, state[g].astype(BF16), preferred_element_type=F32))
            b_t.append(b_g.T)
        yield

        y_parts = []
        for g in range(GROUPS):
            y_tiles = []
            for p in range(HEADS_PER_GROUP // 2):
                pcols = slice(p * LANES, (p + 1) * LANES)
                scores, b_scaled, e_acs = [], [], []
                for q in range(2):
                    hd = g * HEADS_PER_GROUP + 2 * p + q
                    acs_b = jnp.broadcast_to(a_cs[:, hd:hd + 1], (CHUNK, LANES))
                    seg = acs_b - tr[hd:hd + 1, :]
                    scores.append(
                        (cb[g] * jnp.exp2(jnp.where(causal, seg, -jnp.inf))).astype(BF16))
                    b_scaled.append((b_t[g] * tr[HEADS + hd:HEADS + hd + 1, :]).astype(BF16))
                    e_acs.append(jnp.exp2(acs_b))
                tcol = g * (HEADS_PER_GROUP // 2) + p
                rhs = pair_rhs(xc_ref[rows, tcol * LANES:(tcol + 1) * LANES])
                lhs = jnp.concatenate([jnp.concatenate(scores, axis=1),
                                       jnp.concatenate(b_scaled, axis=1)], axis=0)
                res = jnp.dot(lhs, rhs, preferred_element_type=F32)
                decay = jnp.where(low_half, e_acs[0], e_acs[1])
                y_tiles.append(res[0:CHUNK, :] + y_off[g][:, pcols] * decay)
                state_ref[g, :, pcols] = (state[g][:, pcols] * decay[CHUNK - 1:CHUNK, :]
                                          + res[CHUNK:2 * CHUNK, :])
            y_parts.append(jnp.concatenate(y_tiles, axis=1))

        for g in range(GROUPS):
            gcols = slice(g * GROUP_WIDTH, (g + 1) * GROUP_WIDTH)
            zs = proj_ref[rows, Z_COL + g * GROUP_WIDTH:Z_COL + (g + 1) * GROUP_WIDTH]
            yg = (y_parts[g] + dskip_ref[:, gcols] * xc_ref[rows, gcols]) * _silu(zs)
            yn = yg * lax.rsqrt(jnp.mean(yg * yg, axis=-1, keepdims=True) + EPS)
            mix_ref[rows, GM_WIDTH + g * GROUP_WIDTH:GM_WIDTH + (g + 1) * GROUP_WIDTH] = (
                yn * nrmw_ref[:, gcols]).astype(BF16)
        yield

    def mixers(r, fill):
        chains = [gmlp(r), ssd(r)]
        while chains:
            for chain in list(chains):
                try:
                    next(chain)
                except StopIteration:
                    chains.remove(chain)
                    continue
                fill()

    half_rows = BLOCK // 2
    ff_blocks = list(range(0, D_FF, FF_BLOCK))
    cons = [dict() for _ in range(2)]

    def out_proj_piece(hf):
        hrows = slice(hf * half_rows, (hf + 1) * half_rows)
        o = jnp.dot(mix_ref[hrows, :], wout_ref[...], preferred_element_type=F32)
        x1 = xb_ref[hrows, :] + _rms(o, npost_ref[...])
        cons[hf]["x1"] = x1
        cons[hf]["h2"] = _rms(x1, npre_ref[...]).astype(BF16)

    def up_piece(hf, f0):
        up = jnp.dot(cons[hf]["h2"], wup_ref[:, f0:f0 + FF_BLOCK], preferred_element_type=F32)
        cons[hf][f0] = jnp.square(jnp.maximum(up, 0.0)).astype(BF16)

    def down_piece(hf, f0):
        dn = jnp.dot(cons[hf][f0], wdown_ref[f0:f0 + FF_BLOCK, :], preferred_element_type=F32)
        cons[hf]["acc"] = dn if "acc" not in cons[hf] else cons[hf]["acc"] + dn
        if f0 == ff_blocks[-1]:
            hrows = slice(hf * half_rows, (hf + 1) * half_rows)
            out_ref[hrows, :] = cons[hf]["x1"] + _rms(cons[hf]["acc"], nffn_ref[...])

    out_proj_piece(0)
    queue = [functools.partial(out_proj_piece, 1)]
    for hf in range(2):
        queue += [functools.partial(up_piece, hf, f0) for f0 in ff_blocks]
        queue += [functools.partial(down_piece, hf, f0) for f0 in ff_blocks]
    per_slot = [5, 4, 4, 4]
    assert sum(per_slot) == len(queue)

    upper = slice(half_rows, BLOCK)
    proj_queue = (in_proj_pieces(xa_ref[upper, :], upper)
                  + in_proj_pieces(xn_ref[...], slice(0, half_rows)))
    proj_per_slot = len(proj_queue) // n_chunks

    for r in range(n_chunks):
        mine = [queue.pop(0) for _ in range(per_slot[r])]
        nxt = [proj_queue.pop(0) for _ in range(proj_per_slot)]
        fillers = []
        while mine or nxt:
            if mine:
                fillers.append(mine.pop(0))
            if nxt:
                fillers.append(nxt.pop(0))

        def fill(fillers=fillers):
            if fillers:
                fillers.pop(0)()

        mixers(r, fill)
        while fillers:
            fill()

    xpad_ref[0:PAD_ROWS, :] = xpad_ref[BLOCK:BLOCK + PAD_ROWS, :]


def _layer(x2d, bsz, seq, norm_mix_pre, w_in, gm_ln_w, gm_ln_b, gm_w_s, gm_b_s, conv_w, conv_b,
           dt_bias, a_log, d_skip, ssm_norm_w, w_out, norm_mix_post, norm_ffn_pre, w_up, w_down,
           norm_ffn_post):
    t = bsz * seq
    assert seq % BLOCK == 0, (seq, BLOCK)
    n_blocks = t // BLOCK
    row = lambda p: p.reshape(1, -1).astype(F32)

    w_main = w_in[:, :MAIN_COLS].astype(BF16)
    w_dt = jnp.pad(w_in[:, MAIN_COLS:], ((0, 0), (0, LANES - HEADS))).astype(BF16)
    causal = jnp.tril(jnp.ones((CHUNK, CHUNK), dtype=bool))
    w_s = jnp.where(causal[None], gm_w_s, 0.0).astype(BF16)
    wcat = w_s.reshape(HEADS // 2, 2, CHUNK, CHUNK).transpose(0, 2, 1, 3).reshape(
        HEADS // 2, CHUNK, 2 * CHUNK)
    bs_e = jnp.repeat(gm_b_s.T.astype(F32), HEAD_DIM, axis=1)
    dtb = jnp.pad(dt_bias.astype(F32), (0, LANES - HEADS)).reshape(1, LANES)
    a_pad = jnp.pad(-jnp.exp(a_log.astype(F32)), (0, LANES - HEADS)).reshape(1, LANES)
    dskip_e = jnp.repeat(d_skip.astype(F32), HEAD_DIM).reshape(1, SSM_WIDTH)
    head_of_col = jnp.arange(MEAN_WIDTH) // HEAD_DIM
    mavg = jnp.where(head_of_col[:, None] == head_of_col[None, :], 1.0 / HEAD_DIM, 0.0).astype(BF16)
    tri = causal.astype(BF16)

    def const(shape):
        zeros = (0,) * len(shape)
        return pl.BlockSpec(shape, lambda i: zeros, pipeline_mode=pl.Buffered(1))

    last = n_blocks - 1
    last_half = 2 * n_blocks - 1
    in_specs = [
        pl.BlockSpec((BLOCK, D_MODEL), lambda i: (jnp.minimum(i, last), 0)),
        pl.BlockSpec((BLOCK, D_MODEL), lambda i: (jnp.maximum(i - 1, 0), 0)),
        pl.BlockSpec((BLOCK // 2, D_MODEL),
                     lambda i: (jnp.minimum(2 * (i + 1), last_half), 0)),
        const((1, D_MODEL)),
        const((D_MODEL, MAIN_COLS)),
        const((D_MODEL, LANES)),
        const((1, GM_WIDTH)),
        const((1, GM_WIDTH)),
        const((HEADS // 2, CHUNK, 2 * CHUNK)),
        const((CHUNK, GM_WIDTH)),
        const((CONV, CONV_CH)),
        const((1, CONV_CH)),
        const((1, LANES)),
        const((1, LANES)),
        const((1, SSM_WIDTH)),
        const((1, SSM_WIDTH)),
        const((MEAN_WIDTH, MEAN_WIDTH)),
        const((CHUNK, CHUNK)),
        const((MIX_WIDTH, D_MODEL)),
        const((1, D_MODEL)),
        const((1, D_MODEL)),
        const((D_MODEL, D_FF)),
        const((D_FF, D_MODEL)),
        const((1, D_MODEL)),
    ]
    return pl.pallas_call(
        functools.partial(_layer_kernel, blocks_per_seq=seq // BLOCK, n_blocks=n_blocks),
        grid=(n_blocks + 1,),
        in_specs=in_specs,
        out_specs=pl.BlockSpec((BLOCK, D_MODEL), lambda i: (jnp.maximum(i - 1, 0), 0)),
        out_shape=jax.ShapeDtypeStruct((t, D_MODEL), F32),
        scratch_shapes=[
            pltpu.VMEM((BLOCK, MAIN_COLS), F32),
            pltpu.VMEM((BLOCK, LANES), F32),
            pltpu.VMEM((BLOCK, MIX_WIDTH), BF16),
            pltpu.VMEM((BLOCK + PAD_ROWS, CONV_CH), F32),
            pltpu.VMEM((GROUPS, STATE, GROUP_WIDTH), F32),
            pltpu.VMEM((BLOCK, CONV_CH), F32),
        ],
        compiler_params=pltpu.CompilerParams(
            dimension_semantics=("arbitrary",), vmem_limit_bytes=VMEM_LIMIT_BYTES),
        name="hybrid_layer",
    )(x2d, x2d, x2d, row(norm_mix_pre), w_main, w_dt, row(gm_ln_w), row(gm_ln_b), wcat, bs_e,
      conv_w.astype(F32), row(conv_b), dtb, a_pad, dskip_e, row(ssm_norm_w), mavg, tri,
      w_out.astype(BF16), row(norm_mix_post), row(norm_ffn_pre), w_up.astype(BF16),
      w_down.astype(BF16), row(norm_ffn_post))


def kernel(x, norm_mix_pre, w_in, gm_ln_w, gm_ln_b, gm_w_s, gm_b_s, conv_w, conv_b, dt_bias, a_log, d_skip, ssm_norm_w, w_out, norm_mix_post, norm_ffn_pre, w_up, w_down, norm_ffn_post):
    bsz, seq, d = x.shape
    depth = w_in.shape[0]
    x2d = x.reshape(bsz * seq, d)
    for i in range(depth):
        x2d = _layer(x2d, bsz, seq, norm_mix_pre[i], w_in[i], gm_ln_w[i], gm_ln_b[i], gm_w_s[i],
                     gm_b_s[i], conv_w[i], conv_b[i], dt_bias[i], a_log[i], d_skip[i],
                     ssm_norm_w[i], w_out[i], norm_mix_post[i], norm_ffn_pre[i], w_up[i],
                     w_down[i], norm_ffn_post[i])
    return x2d.reshape(bsz, seq, d)
```

```python
import functools
import math

import jax
import jax.numpy as jnp
from jax import lax
from jax.experimental import pallas as pl
from jax.experimental.pallas import tpu as pltpu

F32 = jnp.float32
BF16 = jnp.bfloat16

D_MODEL = 1024
GM_WIDTH = 512
HEAD_DIM = 64
HEADS = 8
CHUNK = 128
SSM_WIDTH = 512
GROUPS = 2
HEADS_PER_GROUP = HEADS // GROUPS
GROUP_WIDTH = HEADS_PER_GROUP * HEAD_DIM
STATE = 128
CONV = 4
CONV_CH = SSM_WIDTH + 2 * GROUPS * STATE
D_FF = 4 * D_MODEL
EPS = 1e-6
MAIN_COLS = 2 * GM_WIDTH + SSM_WIDTH + CONV_CH
Z_COL = 2 * GM_WIDTH
XBC_COL = Z_COL + SSM_WIDTH
MIX_WIDTH = GM_WIDTH + SSM_WIDTH
LANES = 128
MEAN_WIDTH = 256
PAD_ROWS = 8
LOG2E = 1.4426950408889634

VMEM_LIMIT_BYTES = 60 * 1024 * 1024

BLOCK = 512
PROJ_BLOCK = 512
FF_BLOCK = 1024


def _rms(x, w):
    return x * lax.rsqrt(jnp.mean(x * x, axis=-1, keepdims=True) + EPS) * w


def _split_dot_left(m, a, passes):
    pieces = []
    rem = a
    for _ in range(passes):
        piece = rem.astype(BF16)
        pieces.append(piece)
        rem = rem - piece.astype(F32)
    d = jnp.dot(m, jnp.concatenate(pieces, axis=1), preferred_element_type=F32)
    width = a.shape[1]
    acc = d[:, 0:width]
    for k in range(1, passes):
        acc = acc + d[:, k * width:(k + 1) * width]
    return acc


_GELU_A = -2.0 * math.sqrt(2.0 / math.pi) * LOG2E
_GELU_B = _GELU_A * 0.044715


def _gelu(x):
    return x / (1.0 + jnp.exp2(x * (_GELU_A + _GELU_B * (x * x))))


def _silu(x):
    return x / (1.0 + jnp.exp2(x * (-LOG2E)))


def _softplus(x):
    return jnp.maximum(x, 0.0) + jnp.log1p(jnp.exp(-jnp.abs(x)))


def _layer_kernel(
        xa_ref, xb_ref, xn_ref, nmix_ref, win_ref, wdt_ref, lnw_ref, lnb_ref, wcat_ref, bs_ref,
        convw_ref, convb_ref, dtb_ref, a_ref, dskip_ref, nrmw_ref, mavg_ref, tri_ref,
        wout_ref, npost_ref, npre_ref, wup_ref, wdown_ref, nffn_ref,
        out_ref,
        proj_ref, dtraw_ref, mix_ref, xpad_ref, state_ref, xc_ref,
        *, blocks_per_seq, n_blocks):
    i = pl.program_id(0)
    n_chunks = BLOCK // CHUNK
    first_of_seq = lax.rem(jnp.minimum(i, n_blocks - 1), blocks_per_seq) == 0

    def in_proj_pieces(xv, dst):
        hn = _rms(xv, nmix_ref[...]).astype(BF16)

        def main(n0):
            proj_ref[dst, n0:n0 + PROJ_BLOCK] = jnp.dot(
                hn, win_ref[:, n0:n0 + PROJ_BLOCK], preferred_element_type=F32)

        def dt():
            dtraw_ref[dst, :] = jnp.dot(hn, wdt_ref[...], preferred_element_type=F32)

        return [functools.partial(main, n0) for n0 in range(0, MAIN_COLS, PROJ_BLOCK)] + [dt]

    @pl.when(i == 0)
    def _():
        mix_ref[...] = jnp.zeros(mix_ref.shape, BF16)
        for piece in in_proj_pieces(xa_ref[0:BLOCK // 2, :], slice(0, BLOCK // 2)):
            piece()

    @pl.when(first_of_seq)
    def _():
        xpad_ref[0:PAD_ROWS, :] = jnp.zeros((PAD_ROWS, CONV_CH), F32)
        state_ref[...] = jnp.zeros(state_ref.shape, F32)

    lane = lax.broadcasted_iota(jnp.int32, (CHUNK, LANES), 1)
    row = lax.broadcasted_iota(jnp.int32, (CHUNK, LANES), 0)
    causal = row >= lane
    low_half = lane < HEAD_DIM

    def pair_rhs(tile):
        return jnp.concatenate([jnp.where(low_half, tile, 0.0).astype(BF16),
                                jnp.where(low_half, 0.0, tile).astype(BF16)], axis=0)

    def gmlp(r):
        rows = slice(r * CHUNK, (r + 1) * CHUNK)

        def head_mean(a):
            ab = a.astype(BF16)
            return jnp.concatenate(
                [jnp.dot(ab[:, c0:c0 + MEAN_WIDTH], mavg_ref[...], preferred_element_type=F32)
                 for c0 in range(0, GM_WIDTH, MEAN_WIDTH)], axis=1)

        v = _gelu(proj_ref[rows, GM_WIDTH:2 * GM_WIDTH])
        mu = head_mean(v)
        yield
        d = v - mu
        var = head_mean(d * d)
        yield
        vn = d * lax.rsqrt(var + EPS) * lnw_ref[...] + lnb_ref[...]
        for k in range(GM_WIDTH // LANES):
            cols = slice(k * LANES, (k + 1) * LANES)
            mixed = jnp.dot(wcat_ref[k], pair_rhs(vn[:, cols]),
                            preferred_element_type=F32) + bs_ref[:, cols]
            mix_ref[rows, cols] = (_gelu(proj_ref[rows, cols]) * mixed).astype(BF16)
        yield

    def ssd(r):
        r0 = r * CHUNK
        rows = slice(r0, r0 + CHUNK)
        prows = slice(r0 + PAD_ROWS, r0 + PAD_ROWS + CHUNK)

        dt = _softplus(dtraw_ref[rows, :] + dtb_ref[...])
        dta = dt * (a_ref[...] * LOG2E)
        a_cs = _split_dot_left(tri_ref[...], dta, 3)
        yield

        xpad_ref[prows, :] = proj_ref[rows, XBC_COL:MAIN_COLS]
        for ct in range(CONV_CH // LANES):
            cols = slice(ct * LANES, (ct + 1) * LANES)
            acc = convb_ref[:, cols] + convw_ref[CONV - 1:CONV, cols] * xpad_ref[prows, cols]
            for k in range(1, CONV):
                acc = acc + convw_ref[CONV - 1 - k:CONV - k, cols] * xpad_ref[
                    r0 + PAD_ROWS - k:r0 + PAD_ROWS - k + CHUNK, cols]
            xc_ref[rows, cols] = _silu(acc)

        total = a_cs[CHUNK - 1:CHUNK, :]
        w_end = dt * jnp.exp2(total - a_cs)
        tr = jnp.where(lane < HEADS, a_cs - jnp.log(dt) * LOG2E, pltpu.roll(w_end, HEADS, 1)).T

        b_t, cb, y_off, state = [], [], [], []
        for g in range(GROUPS):
            b_g = xc_ref[rows, SSM_WIDTH + g * STATE:SSM_WIDTH + (g + 1) * STATE]
            c_g = xc_ref[rows, SSM_WIDTH + (GROUPS + g) * STATE:
                         SSM_WIDTH + (GROUPS + g + 1) * STATE].astype(BF16)
            cb.append(lax.dot_general(c_g, b_g.astype(BF16), (((1,), (1,)), ((), ())),
                                      preferred_element_type=F32))
            state.append(state_ref[g])
            y_off.append(jnp.dot(c_g, state[g].astype(BF16), preferred_element_type=F32))
            b_t.append(b_g.T)
        yield

        y_parts = []
        for g in range(GROUPS):
            y_tiles = []
            for p in range(HEADS_PER_GROUP // 2):
                pcols = slice(p * LANES, (p + 1) * LANES)
                scores, b_scaled, e_acs = [], [], []
                for q in range(2):
                    hd = g * HEADS_PER_GROUP + 2 * p + q
                    acs_b = jnp.broadcast_to(a_cs[:, hd:hd + 1], (CHUNK, LANES))
                    seg = acs_b - tr[hd:hd + 1, :]
                    scores.append(
                        (cb[g] * jnp.exp2(jnp.where(causal, seg, -jnp.inf))).astype(BF16))
                    b_scaled.append((b_t[g] * tr[HEADS + hd:HEADS + hd + 1, :]).astype(BF16))
                    e_acs.append(jnp.exp2(acs_b))
                tcol = g * (HEADS_PER_GROUP // 2) + p
                rhs = pair_rhs(xc_ref[rows, tcol * LANES:(tcol + 1) * LANES])
                lhs = jnp.concatenate([jnp.concatenate(scores, axis=1),
                                       jnp.concatenate(b_scaled, axis=1)], axis=0)
                res = jnp.dot(lhs, rhs, preferred_element_type=F32)
                decay = jnp.where(low_half, e_acs[0], e_acs[1])
                y_tiles.append(res[0:CHUNK, :] + y_off[g][:, pcols] * decay)
                state_ref[g, :, pcols] = (state[g][:, pcols] * decay[CHUNK - 1:CHUNK, :]
                                          + res[CHUNK:2 * CHUNK, :])
            y_parts.append(jnp.concatenate(y_tiles, axis=1))

        for g in range(GROUPS):
            gcols = slice(g * GROUP_WIDTH, (g + 1) * GROUP_WIDTH)
            zs = proj_ref[rows, Z_COL + g * GROUP_WIDTH:Z_COL + (g + 1) * GROUP_WIDTH]
            yg = (y_parts[g] + dskip_ref[:, gcols] * xc_ref[rows, gcols]) * _silu(zs)
            yn = yg * lax.rsqrt(jnp.mean(yg * yg, axis=-1, keepdims=True) + EPS)
            mix_ref[rows, GM_WIDTH + g * GROUP_WIDTH:GM_WIDTH + (g + 1) * GROUP_WIDTH] = (
                yn * nrmw_ref[:, gcols]).astype(BF16)
        yield

    def mixers(r, fill):
        chains = [gmlp(r), ssd(r)]
        while chains:
            for chain in list(chains):
                try:
                    next(chain)
                except StopIteration:
                    chains.remove(chain)
                    continue
                fill()

    half_rows = BLOCK // 2
    ff_blocks = list(range(0, D_FF, FF_BLOCK))
    cons = [dict() for _ in range(2)]

    def out_proj_piece(hf):
        hrows = slice(hf * half_rows, (hf + 1) * half_rows)
        o = jnp.dot(mix_ref[hrows, :], wout_ref[...], preferred_element_type=F32)
        x1 = xb_ref[hrows, :] + _rms(o, npost_ref[...])
        cons[hf]["x1"] = x1
        cons[hf]["h2"] = _rms(x1, npre_ref[...]).astype(BF16)

    FFP = 512
    ff_pieces = list(range(0, D_FF, FFP))
    full = {}

    def up_piece(f0):
        h2 = jnp.concatenate([cons[0]["h2"], cons[1]["h2"]], axis=0)
        up = jnp.dot(h2, wup_ref[:, f0:f0 + FFP], preferred_element_type=F32)
        full[f0] = jnp.square(jnp.maximum(up, 0.0)).astype(BF16)

    def down_piece(f0):
        dn = jnp.dot(full[f0], wdown_ref[f0:f0 + FFP, :], preferred_element_type=F32)
        full["acc"] = dn if "acc" not in full else full["acc"] + dn
        if f0 == ff_pieces[-1]:
            for hf in range(2):
                hrows = slice(hf * half_rows, (hf + 1) * half_rows)
                out_ref[hrows, :] = cons[hf]["x1"] + _rms(full["acc"][hrows, :], nffn_ref[...])

    out_proj_piece(0)
    queue = [functools.partial(out_proj_piece, 1)]
    queue += [functools.partial(up_piece, f0) for f0 in ff_pieces[:4]]
    queue += [functools.partial(down_piece, f0) for f0 in ff_pieces[:4]]
    queue += [functools.partial(up_piece, f0) for f0 in ff_pieces[4:]]
    queue += [functools.partial(down_piece, f0) for f0 in ff_pieces[4:]]
    per_slot = [5, 4, 4, 4]
    assert sum(per_slot) == len(queue)

    upper = slice(half_rows, BLOCK)
    proj_queue = (in_proj_pieces(xa_ref[upper, :], upper)
                  + in_proj_pieces(xn_ref[...], slice(0, half_rows)))
    proj_per_slot = len(proj_queue) // n_chunks

    for r in range(n_chunks):
        mine = [queue.pop(0) for _ in range(per_slot[r])]
        nxt = [proj_queue.pop(0) for _ in range(proj_per_slot)]
        fillers = []
        while mine or nxt:
            if mine:
                fillers.append(mine.pop(0))
            if nxt:
                fillers.append(nxt.pop(0))

        def fill(fillers=fillers):
            if fillers:
                fillers.pop(0)()

        mixers(r, fill)
        while fillers:
            fill()

    xpad_ref[0:PAD_ROWS, :] = xpad_ref[BLOCK:BLOCK + PAD_ROWS, :]


def _layer(x2d, bsz, seq, norm_mix_pre, w_in, gm_ln_w, gm_ln_b, gm_w_s, gm_b_s, conv_w, conv_b,
           dt_bias, a_log, d_skip, ssm_norm_w, w_out, norm_mix_post, norm_ffn_pre, w_up, w_down,
           norm_ffn_post):
    t = bsz * seq
    assert seq % BLOCK == 0, (seq, BLOCK)
    n_blocks = t // BLOCK
    row = lambda p: p.reshape(1, -1).astype(F32)

    w_main = w_in[:, :MAIN_COLS].astype(BF16)
    w_dt = jnp.pad(w_in[:, MAIN_COLS:], ((0, 0), (0, LANES - HEADS))).astype(BF16)
    causal = jnp.tril(jnp.ones((CHUNK, CHUNK), dtype=bool))
    w_s = jnp.where(causal[None], gm_w_s, 0.0).astype(BF16)
    wcat = w_s.reshape(HEADS // 2, 2, CHUNK, CHUNK).transpose(0, 2, 1, 3).reshape(
        HEADS // 2, CHUNK, 2 * CHUNK)
    bs_e = jnp.repeat(gm_b_s.T.astype(F32), HEAD_DIM, axis=1)
    dtb = jnp.pad(dt_bias.astype(F32), (0, LANES - HEADS)).reshape(1, LANES)
    a_pad = jnp.pad(-jnp.exp(a_log.astype(F32)), (0, LANES - HEADS)).reshape(1, LANES)
    dskip_e = jnp.repeat(d_skip.astype(F32), HEAD_DIM).reshape(1, SSM_WIDTH)
    head_of_col = jnp.arange(MEAN_WIDTH) // HEAD_DIM
    mavg = jnp.where(head_of_col[:, None] == head_of_col[None, :], 1.0 / HEAD_DIM, 0.0).astype(BF16)
    tri = causal.astype(BF16)

    def const(shape):
        zeros = (0,) * len(shape)
        return pl.BlockSpec(shape, lambda i: zeros, pipeline_mode=pl.Buffered(1))

    last = n_blocks - 1
    last_half = 2 * n_blocks - 1
    in_specs = [
        pl.BlockSpec((BLOCK, D_MODEL), lambda i: (jnp.minimum(i, last), 0)),
        pl.BlockSpec((BLOCK, D_MODEL), lambda i: (jnp.maximum(i - 1, 0), 0)),
        pl.BlockSpec((BLOCK // 2, D_MODEL),
                     lambda i: (jnp.minimum(2 * (i + 1), last_half), 0)),
        const((1, D_MODEL)),
        const((D_MODEL, MAIN_COLS)),
        const((D_MODEL, LANES)),
        const((1, GM_WIDTH)),
        const((1, GM_WIDTH)),
        const((HEADS // 2, CHUNK, 2 * CHUNK)),
        const((CHUNK, GM_WIDTH)),
        const((CONV, CONV_CH)),
        const((1, CONV_CH)),
        const((1, LANES)),
        const((1, LANES)),
        const((1, SSM_WIDTH)),
        const((1, SSM_WIDTH)),
        const((MEAN_WIDTH, MEAN_WIDTH)),
        const((CHUNK, CHUNK)),
        const((MIX_WIDTH, D_MODEL)),
        const((1, D_MODEL)),
        const((1, D_MODEL)),
        const((D_MODEL, D_FF)),
        const((D_FF, D_MODEL)),
        const((1, D_MODEL)),
    ]
    return pl.pallas_call(
        functools.partial(_layer_kernel, blocks_per_seq=seq // BLOCK, n_blocks=n_blocks),
        grid=(n_blocks + 1,),
        in_specs=in_specs,
        out_specs=pl.BlockSpec((BLOCK, D_MODEL), lambda i: (jnp.maximum(i - 1, 0), 0)),
        out_shape=jax.ShapeDtypeStruct((t, D_MODEL), F32),
        scratch_shapes=[
            pltpu.VMEM((BLOCK, MAIN_COLS), F32),
            pltpu.VMEM((BLOCK, LANES), F32),
            pltpu.VMEM((BLOCK, MIX_WIDTH), BF16),
            pltpu.VMEM((BLOCK + PAD_ROWS, CONV_CH), F32),
            pltpu.VMEM((GROUPS, STATE, GROUP_WIDTH), F32),
            pltpu.VMEM((BLOCK, CONV_CH), F32),
        ],
        compiler_params=pltpu.CompilerParams(
            dimension_semantics=("arbitrary",), vmem_limit_bytes=VMEM_LIMIT_BYTES),
        name="hybrid_layer",
    )(x2d, x2d, x2d, row(norm_mix_pre), w_main, w_dt, row(gm_ln_w), row(gm_ln_b), wcat, bs_e,
      conv_w.astype(F32), row(conv_b), dtb, a_pad, dskip_e, row(ssm_norm_w), mavg, tri,
      w_out.astype(BF16), row(norm_mix_post), row(norm_ffn_pre), w_up.astype(BF16),
      w_down.astype(BF16), row(norm_ffn_post))


def kernel(x, norm_mix_pre, w_in, gm_ln_w, gm_ln_b, gm_w_s, gm_b_s, conv_w, conv_b, dt_bias, a_log, d_skip, ssm_norm_w, w_out, norm_mix_post, norm_ffn_pre, w_up, w_down, norm_ffn_post):
    bsz, seq, d = x.shape
    depth = w_in.shape[0]
    x2d = x.reshape(bsz * seq, d)
    for i in range(depth):
        x2d = _layer(x2d, bsz, seq, norm_mix_pre[i], w_in[i], gm_ln_w[i], gm_ln_b[i], gm_w_s[i],
                     gm_b_s[i], conv_w[i], conv_b[i], dt_bias[i], a_log[i], d_skip[i],
                     ssm_norm_w[i], w_out[i], norm_mix_post[i], norm_ffn_pre[i], w_up[i],
                     w_down[i], norm_ffn_post[i])
    return x2d.reshape(bsz, seq, d)
```

```python
import functools
import math

import jax
import jax.numpy as jnp
from jax import lax
from jax.experimental import pallas as pl
from jax.experimental.pallas import tpu as pltpu

F32 = jnp.float32
BF16 = jnp.bfloat16

D_MODEL = 1024
GM_WIDTH = 512
HEAD_DIM = 64
HEADS = 8
CHUNK = 128
SSM_WIDTH = 512
GROUPS = 2
HEADS_PER_GROUP = HEADS // GROUPS
GROUP_WIDTH = HEADS_PER_GROUP * HEAD_DIM
STATE = 128
CONV = 4
CONV_CH = SSM_WIDTH + 2 * GROUPS * STATE
D_FF = 4 * D_MODEL
EPS = 1e-6
MAIN_COLS = 2 * GM_WIDTH + SSM_WIDTH + CONV_CH
Z_COL = 2 * GM_WIDTH
XBC_COL = Z_COL + SSM_WIDTH
MIX_WIDTH = GM_WIDTH + SSM_WIDTH
LANES = 128
MEAN_WIDTH = 256
PAD_ROWS = 8
LOG2E = 1.4426950408889634

VMEM_LIMIT_BYTES = 60 * 1024 * 1024

BLOCK = 512
PROJ_BLOCK = 512
FF_PIECE = 512


def _rms(x, w):
    return x * lax.rsqrt(jnp.mean(x * x, axis=-1, keepdims=True) + EPS) * w


def _split_dot_left(m, a, passes):
    pieces = []
    rem = a
    for _ in range(passes):
        piece = rem.astype(BF16)
        pieces.append(piece)
        rem = rem - piece.astype(F32)
    d = jnp.dot(m, jnp.concatenate(pieces, axis=1), preferred_element_type=F32)
    width = a.shape[1]
    acc = d[:, 0:width]
    for k in range(1, passes):
        acc = acc + d[:, k * width:(k + 1) * width]
    return acc


_GELU_A = -2.0 * math.sqrt(2.0 / math.pi) * LOG2E
_GELU_B = _GELU_A * 0.044715


def _gelu(x):
    return x / (1.0 + jnp.exp2(x * (_GELU_A + _GELU_B * (x * x))))


def _silu(x):
    return x / (1.0 + jnp.exp2(x * (-LOG2E)))


def _softplus(x):
    return jnp.maximum(x, 0.0) + jnp.log1p(jnp.exp(-jnp.abs(x)))


def _layer_kernel(
        xa_ref, xb_ref, xn_ref, nmix_ref, win_ref, wdt_ref, lnw_ref, lnb_ref, wcat_ref, bs_ref,
        convw_ref, convb_ref, dtb_ref, a_ref, dskip_ref, nrmw_ref, mavg_ref, tri_ref,
        wout_ref, npost_ref, npre_ref, wup_ref, wdown_ref, nffn_ref,
        out_ref,
        proj_ref, dtraw_ref, mix_ref, xpad_ref, state_ref, xc_ref,
        *, blocks_per_seq, n_blocks):
    i = pl.program_id(0)
    n_chunks = BLOCK // CHUNK
    first_of_seq = lax.rem(jnp.minimum(i, n_blocks - 1), blocks_per_seq) == 0

    def in_proj_pieces(xv, dst):
        hn = _rms(xv, nmix_ref[...]).astype(BF16)

        def main(n0):
            proj_ref[dst, n0:n0 + PROJ_BLOCK] = jnp.dot(
                hn, win_ref[:, n0:n0 + PROJ_BLOCK], preferred_element_type=F32)

        def dt():
            dtraw_ref[dst, :] = jnp.dot(hn, wdt_ref[...], preferred_element_type=F32)

        return [functools.partial(main, n0) for n0 in range(0, MAIN_COLS, PROJ_BLOCK)] + [dt]

    @pl.when(i == 0)
    def _():
        mix_ref[...] = jnp.zeros(mix_ref.shape, BF16)
        for piece in in_proj_pieces(xa_ref[0:BLOCK // 2, :], slice(0, BLOCK // 2)):
            piece()

    @pl.when(first_of_seq)
    def _():
        xpad_ref[0:PAD_ROWS, :] = jnp.zeros((PAD_ROWS, CONV_CH), F32)
        state_ref[...] = jnp.zeros(state_ref.shape, F32)

    lane = lax.broadcasted_iota(jnp.int32, (CHUNK, LANES), 1)
    row = lax.broadcasted_iota(jnp.int32, (CHUNK, LANES), 0)
    causal = row >= lane
    low_half = lane < HEAD_DIM

    def pair_rhs(tile):
        return jnp.concatenate([jnp.where(low_half, tile, 0.0).astype(BF16),
                                jnp.where(low_half, 0.0, tile).astype(BF16)], axis=0)

    def gmlp(r):
        rows = slice(r * CHUNK, (r + 1) * CHUNK)

        def head_mean(a):
            ab = a.astype(BF16)
            col_blocks = range(0, GM_WIDTH, MEAN_WIDTH)
            stacked = jnp.concatenate([ab[:, c0:c0 + MEAN_WIDTH] for c0 in col_blocks], axis=0)
            m = jnp.dot(stacked, mavg_ref[...], preferred_element_type=F32)
            return jnp.concatenate(
                [m[k * CHUNK:(k + 1) * CHUNK, :] for k in range(len(col_blocks))], axis=1)

        v = _gelu(proj_ref[rows, GM_WIDTH:2 * GM_WIDTH])
        mu = head_mean(v)
        yield
        d = v - mu
        var = head_mean(d * d)
        yield
        vn = d * lax.rsqrt(var + EPS) * lnw_ref[...] + lnb_ref[...]
        for k in range(GM_WIDTH // LANES):
            cols = slice(k * LANES, (k + 1) * LANES)
            mixed = jnp.dot(wcat_ref[k], pair_rhs(vn[:, cols]),
                            preferred_element_type=F32) + bs_ref[:, cols]
            mix_ref[rows, cols] = (_gelu(proj_ref[rows, cols]) * mixed).astype(BF16)
        yield

    def ssd(r):
        r0 = r * CHUNK
        rows = slice(r0, r0 + CHUNK)
        prows = slice(r0 + PAD_ROWS, r0 + PAD_ROWS + CHUNK)

        dt = _softplus(dtraw_ref[rows, :] + dtb_ref[...])
        dta = dt * (a_ref[...] * LOG2E)
        a_cs = _split_dot_left(tri_ref[...], dta, 3)
        yield

        xpad_ref[prows, :] = proj_ref[rows, XBC_COL:MAIN_COLS]
        for ct in range(CONV_CH // LANES):
            cols = slice(ct * LANES, (ct + 1) * LANES)
            acc = convb_ref[:, cols] + convw_ref[CONV - 1:CONV, cols] * xpad_ref[prows, cols]
            for k in range(1, CONV):
                acc = acc + convw_ref[CONV - 1 - k:CONV - k, cols] * xpad_ref[
                    r0 + PAD_ROWS - k:r0 + PAD_ROWS - k + CHUNK, cols]
            xc_ref[rows, cols] = _silu(acc)

        total = a_cs[CHUNK - 1:CHUNK, :]
        w_end = dt * jnp.exp2(total - a_cs)
        tr = jnp.where(lane < HEADS, a_cs - jnp.log(dt) * LOG2E, pltpu.roll(w_end, HEADS, 1)).T

        b_t, cb, y_off, state = [], [], [], []
        for g in range(GROUPS):
            b_g = xc_ref[rows, SSM_WIDTH + g * STATE:SSM_WIDTH + (g + 1) * STATE]
            c_g = xc_ref[rows, SSM_WIDTH + (GROUPS + g) * STATE:
                         SSM_WIDTH + (GROUPS + g + 1) * STATE].astype(BF16)
            cb.append(lax.dot_general(c_g, b_g.astype(BF16), (((1,), (1,)), ((), ())),
                                      preferred_element_type=F32))
            state.append(state_ref[g])
            y_off.append(jnp.dot(c_g, state[g].astype(BF16), preferred_element_type=F32))
            b_t.append(b_g.T)
        yield

        y_parts = []
        for g in range(GROUPS):
            y_tiles = []
            for p in range(HEADS_PER_GROUP // 2):
                pcols = slice(p * LANES, (p + 1) * LANES)
                scores, b_scaled, e_acs = [], [], []
                for q in range(2):
                    hd = g * HEADS_PER_GROUP + 2 * p + q
                    acs_b = jnp.broadcast_to(a_cs[:, hd:hd + 1], (CHUNK, LANES))
                    seg = acs_b - tr[hd:hd + 1, :]
                    scores.append(
                        (cb[g] * jnp.exp2(jnp.where(causal, seg, -jnp.inf))).astype(BF16))
                    b_scaled.append((b_t[g] * tr[HEADS + hd:HEADS + hd + 1, :]).astype(BF16))
                    e_acs.append(jnp.exp2(acs_b))
                tcol = g * (HEADS_PER_GROUP // 2) + p
                rhs = pair_rhs(xc_ref[rows, tcol * LANES:(tcol + 1) * LANES])
                lhs = jnp.concatenate([jnp.concatenate(scores, axis=1),
                                       jnp.concatenate(b_scaled, axis=1)], axis=0)
                res = jnp.dot(lhs, rhs, preferred_element_type=F32)
                decay = jnp.where(low_half, e_acs[0], e_acs[1])
                y_tiles.append(res[0:CHUNK, :] + y_off[g][:, pcols] * decay)
                state_ref[g, :, pcols] = (state[g][:, pcols] * decay[CHUNK - 1:CHUNK, :]
                                          + res[CHUNK:2 * CHUNK, :])
            y_parts.append(jnp.concatenate(y_tiles, axis=1))

        for g in range(GROUPS):
            gcols = slice(g * GROUP_WIDTH, (g + 1) * GROUP_WIDTH)
            zs = proj_ref[rows, Z_COL + g * GROUP_WIDTH:Z_COL + (g + 1) * GROUP_WIDTH]
            yg = (y_parts[g] + dskip_ref[:, gcols] * xc_ref[rows, gcols]) * _silu(zs)
            yn = yg * lax.rsqrt(jnp.mean(yg * yg, axis=-1, keepdims=True) + EPS)
            mix_ref[rows, GM_WIDTH + g * GROUP_WIDTH:GM_WIDTH + (g + 1) * GROUP_WIDTH] = (
                yn * nrmw_ref[:, gcols]).astype(BF16)
        yield

    def mixers(r, fill):
        chains = [gmlp(r), ssd(r)]
        while chains:
            for chain in list(chains):
                try:
                    next(chain)
                except StopIteration:
                    chains.remove(chain)
                    continue
                fill()

    half_rows = BLOCK // 2
    cons = [dict() for _ in range(2)]

    def out_proj_piece(hf):
        hrows = slice(hf * half_rows, (hf + 1) * half_rows)
        o = jnp.dot(mix_ref[hrows, :], wout_ref[...], preferred_element_type=F32)
        x1 = xb_ref[hrows, :] + _rms(o, npost_ref[...])
        cons[hf]["x1"] = x1
        cons[hf]["h2"] = _rms(x1, npre_ref[...]).astype(BF16)

    ff_pieces = list(range(0, D_FF, FF_PIECE))
    full = {}

    def up_piece(f0):
        h2 = jnp.concatenate([cons[0]["h2"], cons[1]["h2"]], axis=0)
        up = jnp.dot(h2, wup_ref[:, f0:f0 + FF_PIECE], preferred_element_type=F32)
        full[f0] = jnp.square(jnp.maximum(up, 0.0)).astype(BF16)

    def down_piece(f0):
        dn = jnp.dot(full[f0], wdown_ref[f0:f0 + FF_PIECE, :], preferred_element_type=F32)
        full["acc"] = dn if "acc" not in full else full["acc"] + dn
        if f0 == ff_pieces[-1]:
            for hf in range(2):
                hrows = slice(hf * half_rows, (hf + 1) * half_rows)
                out_ref[hrows, :] = cons[hf]["x1"] + _rms(full["acc"][hrows, :], nffn_ref[...])

    out_proj_piece(0)
    queue = [functools.partial(out_proj_piece, 1)]
    mid = len(ff_pieces) // 2
    for part in (ff_pieces[:mid], ff_pieces[mid:]):
        queue += [functools.partial(up_piece, f0) for f0 in part]
        queue += [functools.partial(down_piece, f0) for f0 in part]
    per_slot = [5, 4, 4, 4]
    assert sum(per_slot) == len(queue)

    upper = slice(half_rows, BLOCK)
    proj_queue = (in_proj_pieces(xa_ref[upper, :], upper)
                  + in_proj_pieces(xn_ref[...], slice(0, half_rows)))
    proj_per_slot = len(proj_queue) // n_chunks

    for r in range(n_chunks):
        mine = [queue.pop(0) for _ in range(per_slot[r])]
        nxt = [proj_queue.pop(0) for _ in range(proj_per_slot)]
        fillers = []
        while mine or nxt:
            if mine:
                fillers.append(mine.pop(0))
            if nxt:
                fillers.append(nxt.pop(0))

        def fill(fillers=fillers):
            if fillers:
                fillers.pop(0)()

        mixers(r, fill)
        while fillers:
            fill()

    xpad_ref[0:PAD_ROWS, :] = xpad_ref[BLOCK:BLOCK + PAD_ROWS, :]


def _layer(x2d, bsz, seq, norm_mix_pre, w_in, gm_ln_w, gm_ln_b, gm_w_s, gm_b_s, conv_w, conv_b,
           dt_bias, a_log, d_skip, ssm_norm_w, w_out, norm_mix_post, norm_ffn_pre, w_up, w_down,
           norm_ffn_post):
    t = bsz * seq
    assert seq % BLOCK == 0, (seq, BLOCK)
    n_blocks = t // BLOCK
    row = lambda p: p.reshape(1, -1).astype(F32)

    w_main = w_in[:, :MAIN_COLS].astype(BF16)
    w_dt = jnp.pad(w_in[:, MAIN_COLS:], ((0, 0), (0, LANES - HEADS))).astype(BF16)
    causal = jnp.tril(jnp.ones((CHUNK, CHUNK), dtype=bool))
    w_s = jnp.where(causal[None], gm_w_s, 0.0).astype(BF16)
    wcat = w_s.reshape(HEADS // 2, 2, CHUNK, CHUNK).transpose(0, 2, 1, 3).reshape(
        HEADS // 2, CHUNK, 2 * CHUNK)
    bs_e = jnp.repeat(gm_b_s.T.astype(F32), HEAD_DIM, axis=1)
    dtb = jnp.pad(dt_bias.astype(F32), (0, LANES - HEADS)).reshape(1, LANES)
    a_pad = jnp.pad(-jnp.exp(a_log.astype(F32)), (0, LANES - HEADS)).reshape(1, LANES)
    dskip_e = jnp.repeat(d_skip.astype(F32), HEAD_DIM).reshape(1, SSM_WIDTH)
    head_of_col = jnp.arange(MEAN_WIDTH) // HEAD_DIM
    mavg = jnp.where(head_of_col[:, None] == head_of_col[None, :], 1.0 / HEAD_DIM, 0.0).astype(BF16)
    tri = causal.astype(BF16)

    def const(shape):
        zeros = (0,) * len(shape)
        return pl.BlockSpec(shape, lambda i: zeros, pipeline_mode=pl.Buffered(1))

    last = n_blocks - 1
    last_half = 2 * n_blocks - 1
    in_specs = [
        pl.BlockSpec((BLOCK, D_MODEL), lambda i: (jnp.minimum(i, last), 0)),
        pl.BlockSpec((BLOCK, D_MODEL), lambda i: (jnp.maximum(i - 1, 0), 0)),
        pl.BlockSpec((BLOCK // 2, D_MODEL),
                     lambda i: (jnp.minimum(2 * (i + 1), last_half), 0)),
        const((1, D_MODEL)),
        const((D_MODEL, MAIN_COLS)),
        const((D_MODEL, LANES)),
        const((1, GM_WIDTH)),
        const((1, GM_WIDTH)),
        const((HEADS // 2, CHUNK, 2 * CHUNK)),
        const((CHUNK, GM_WIDTH)),
        const((CONV, CONV_CH)),
        const((1, CONV_CH)),
        const((1, LANES)),
        const((1, LANES)),
        const((1, SSM_WIDTH)),
        const((1, SSM_WIDTH)),
        const((MEAN_WIDTH, MEAN_WIDTH)),
        const((CHUNK, CHUNK)),
        const((MIX_WIDTH, D_MODEL)),
        const((1, D_MODEL)),
        const((1, D_MODEL)),
        const((D_MODEL, D_FF)),
        const((D_FF, D_MODEL)),
        const((1, D_MODEL)),
    ]
    return pl.pallas_call(
        functools.partial(_layer_kernel, blocks_per_seq=seq // BLOCK, n_blocks=n_blocks),
        grid=(n_blocks + 1,),
        in_specs=in_specs,
        out_specs=pl.BlockSpec((BLOCK, D_MODEL), lambda i: (jnp.maximum(i - 1, 0), 0)),
        out_shape=jax.ShapeDtypeStruct((t, D_MODEL), F32),
        scratch_shapes=[
            pltpu.VMEM((BLOCK, MAIN_COLS), F32),
            pltpu.VMEM((BLOCK, LANES), F32),
            pltpu.VMEM((BLOCK, MIX_WIDTH), BF16),
            pltpu.VMEM((BLOCK + PAD_ROWS, CONV_CH), F32),
            pltpu.VMEM((GROUPS, STATE, GROUP_WIDTH), F32),
            pltpu.VMEM((BLOCK, CONV_CH), F32),
        ],
        compiler_params=pltpu.CompilerParams(
            dimension_semantics=("arbitrary",), vmem_limit_bytes=VMEM_LIMIT_BYTES),
        name="hybrid_layer",
    )(x2d, x2d, x2d, row(norm_mix_pre), w_main, w_dt, row(gm_ln_w), row(gm_ln_b), wcat, bs_e,
      conv_w.astype(F32), row(conv_b), dtb, a_pad, dskip_e, row(ssm_norm_w), mavg, tri,
      w_out.astype(BF16), row(norm_mix_post), row(norm_ffn_pre), w_up.astype(BF16),
      w_down.astype(BF16), row(norm_ffn_post))


def kernel(x, norm_mix_pre, w_in, gm_ln_w, gm_ln_b, gm_w_s, gm_b_s, conv_w, conv_b, dt_bias, a_log, d_skip, ssm_norm_w, w_out, norm_mix_post, norm_ffn_pre, w_up, w_down, norm_ffn_post):
    bsz, seq, d = x.shape
    depth = w_in.shape[0]
    x2d = x.reshape(bsz * seq, d)
    for i in range(depth):
        x2d = _layer(x2d, bsz, seq, norm_mix_pre[i], w_in[i], gm_ln_w[i], gm_ln_b[i], gm_w_s[i],
                     gm_b_s[i], conv_w[i], conv_b[i], dt_bias[i], a_log[i], d_skip[i],
                     ssm_norm_w[i], w_out[i], norm_mix_post[i], norm_ffn_pre[i], w_up[i],
                     w_down[i], norm_ffn_post[i])
    return x2d.reshape(bsz, seq, d)
```

```python
import functools
import math

import jax
import jax.numpy as jnp
from jax import lax
from jax.experimental import pallas as pl
from jax.experimental.pallas import tpu as pltpu

F32 = jnp.float32
BF16 = jnp.bfloat16

D_MODEL = 1024
GM_WIDTH = 512
HEAD_DIM = 64
HEADS = 8
CHUNK = 128
SSM_WIDTH = 512
GROUPS = 2
HEADS_PER_GROUP = HEADS // GROUPS
GROUP_WIDTH = HEADS_PER_GROUP * HEAD_DIM
STATE = 128
CONV = 4
CONV_CH = SSM_WIDTH + 2 * GROUPS * STATE
D_FF = 4 * D_MODEL
EPS = 1e-6
MAIN_COLS = 2 * GM_WIDTH + SSM_WIDTH + CONV_CH
Z_COL = 2 * GM_WIDTH
XBC_COL = Z_COL + SSM_WIDTH
MIX_WIDTH = GM_WIDTH + SSM_WIDTH
LANES = 128
MEAN_WIDTH = 256
PAD_ROWS = 8
LOG2E = 1.4426950408889634

VMEM_LIMIT_BYTES = 60 * 1024 * 1024

BLOCK = 512
PROJ_BLOCK = 512
FF_PIECE = 512


def _rms(x, w):
    return x * lax.rsqrt(jnp.mean(x * x, axis=-1, keepdims=True) + EPS) * w


def _split_dot_left(m, a, passes):
    pieces = []
    rem = a
    for _ in range(passes):
        piece = rem.astype(BF16)
        pieces.append(piece)
        rem = rem - piece.astype(F32)
    d = jnp.dot(m, jnp.concatenate(pieces, axis=1), preferred_element_type=F32)
    width = a.shape[1]
    acc = d[:, 0:width]
    for k in range(1, passes):
        acc = acc + d[:, k * width:(k + 1) * width]
    return acc


_GELU_A = -2.0 * math.sqrt(2.0 / math.pi) * LOG2E
_GELU_B = _GELU_A * 0.044715


def _gelu(x):
    return x / (1.0 + jnp.exp2(x * (_GELU_A + _GELU_B * (x * x))))


def _silu(x):
    return x / (1.0 + jnp.exp2(x * (-LOG2E)))


def _softplus(x):
    return jnp.maximum(x, 0.0) + jnp.log1p(jnp.exp(-jnp.abs(x)))


def _layer_kernel(
        xa_ref, xb_ref, xn_ref, nmix_ref, win_ref, wdt_ref, lnw_ref, lnb_ref, wcat_ref, bs_ref,
        convw_ref, convb_ref, dtb_ref, a_ref, dskip_ref, nrmw_ref, mavg_ref, tri_ref,
        wout_ref, npost_ref, npre_ref, wup_ref, wdown_ref, nffn_ref,
        out_ref,
        proj_ref, dtraw_ref, mix_ref, xpad_ref, state_ref, xc_ref,
        *, blocks_per_seq, n_blocks):
    i = pl.program_id(0)
    n_chunks = BLOCK // CHUNK
    first_of_seq = lax.rem(jnp.minimum(i, n_blocks - 1), blocks_per_seq) == 0

    def in_proj_pieces(xv, dst):
        hn = _rms(xv, nmix_ref[...]).astype(BF16)

        def main(n0):
            proj_ref[dst, n0:n0 + PROJ_BLOCK] = jnp.dot(
                hn, win_ref[:, n0:n0 + PROJ_BLOCK], preferred_element_type=F32)

        def dt():
            dtraw_ref[dst, :] = jnp.dot(hn, wdt_ref[...], preferred_element_type=F32)

        return [functools.partial(main, n0) for n0 in range(0, MAIN_COLS, PROJ_BLOCK)] + [dt]

    @pl.when(i == 0)
    def _():
        mix_ref[...] = jnp.zeros(mix_ref.shape, BF16)
        for piece in in_proj_pieces(xa_ref[0:BLOCK // 2, :], slice(0, BLOCK // 2)):
            piece()

    @pl.when(first_of_seq)
    def _():
        xpad_ref[0:PAD_ROWS, :] = jnp.zeros((PAD_ROWS, CONV_CH), F32)
        state_ref[...] = jnp.zeros(state_ref.shape, F32)

    lane = lax.broadcasted_iota(jnp.int32, (CHUNK, LANES), 1)
    row = lax.broadcasted_iota(jnp.int32, (CHUNK, LANES), 0)
    causal = row >= lane
    low_half = lane < HEAD_DIM

    def block_diag(a, b):
        return jnp.concatenate(
            [jnp.concatenate([a, jnp.zeros((a.shape[0], b.shape[1]), a.dtype)], axis=1),
             jnp.concatenate([jnp.zeros((b.shape[0], a.shape[1]), b.dtype), b], axis=1)], axis=0)

    def pair_rhs(tile):
        return jnp.concatenate([jnp.where(low_half, tile, 0.0).astype(BF16),
                                jnp.where(low_half, 0.0, tile).astype(BF16)], axis=0)

    def gmlp(r):
        rows = slice(r * CHUNK, (r + 1) * CHUNK)

        def head_mean(a):
            ab = a.astype(BF16)
            col_blocks = range(0, GM_WIDTH, MEAN_WIDTH)
            stacked = jnp.concatenate([ab[:, c0:c0 + MEAN_WIDTH] for c0 in col_blocks], axis=0)
            m = jnp.dot(stacked, mavg_ref[...], preferred_element_type=F32)
            return jnp.concatenate(
                [m[k * CHUNK:(k + 1) * CHUNK, :] for k in range(len(col_blocks))], axis=1)

        v = _gelu(proj_ref[rows, GM_WIDTH:2 * GM_WIDTH])
        mu = head_mean(v)
        yield
        d = v - mu
        var = head_mean(d * d)
        yield
        vn = d * lax.rsqrt(var + EPS) * lnw_ref[...] + lnb_ref[...]
        for j in range(GM_WIDTH // (2 * LANES)):
            cols = slice(2 * j * LANES, 2 * (j + 1) * LANES)
            rhs = block_diag(pair_rhs(vn[:, 2 * j * LANES:(2 * j + 1) * LANES]),
                             pair_rhs(vn[:, (2 * j + 1) * LANES:(2 * j + 2) * LANES]))
            mixed = jnp.dot(wcat_ref[j], rhs, preferred_element_type=F32) + bs_ref[:, cols]
            mix_ref[rows, cols] = (_gelu(proj_ref[rows, cols]) * mixed).astype(BF16)
        yield

    def ssd(r):
        r0 = r * CHUNK
        rows = slice(r0, r0 + CHUNK)
        prows = slice(r0 + PAD_ROWS, r0 + PAD_ROWS + CHUNK)

        dt = _softplus(dtraw_ref[rows, :] + dtb_ref[...])
        dta = dt * (a_ref[...] * LOG2E)
        a_cs = _split_dot_left(tri_ref[...], dta, 3)
        yield

        xpad_ref[prows, :] = proj_ref[rows, XBC_COL:MAIN_COLS]
        for ct in range(CONV_CH // LANES):
            cols = slice(ct * LANES, (ct + 1) * LANES)
            acc = convb_ref[:, cols] + convw_ref[CONV - 1:CONV, cols] * xpad_ref[prows, cols]
            for k in range(1, CONV):
                acc = acc + convw_ref[CONV - 1 - k:CONV - k, cols] * xpad_ref[
                    r0 + PAD_ROWS - k:r0 + PAD_ROWS - k + CHUNK, cols]
            xc_ref[rows, cols] = _silu(acc)

        total = a_cs[CHUNK - 1:CHUNK, :]
        w_end = dt * jnp.exp2(total - a_cs)
        tr = jnp.where(lane < HEADS, a_cs - jnp.log(dt) * LOG2E, pltpu.roll(w_end, HEADS, 1)).T

        b_t, cb, y_off, state = [], [], [], []
        c_all, rhs_g = [], []
        for g in range(GROUPS):
            b_g = xc_ref[rows, SSM_WIDTH + g * STATE:SSM_WIDTH + (g + 1) * STATE]
            c_all.append(xc_ref[rows, SSM_WIDTH + (GROUPS + g) * STATE:
                                SSM_WIDTH + (GROUPS + g + 1) * STATE].astype(BF16))
            state.append(state_ref[g])
            b_t.append(b_g.T)
            rhs_g.append(jnp.concatenate([b_t[g].astype(BF16), state[g].astype(BF16)], axis=1))
        cy = jnp.dot(jnp.concatenate(c_all, axis=1), block_diag(rhs_g[0], rhs_g[1]),
                     preferred_element_type=F32)
        gw = STATE + GROUP_WIDTH
        for g in range(GROUPS):
            cb.append(cy[:, g * gw:g * gw + STATE])
            y_off.append(cy[:, g * gw + STATE:(g + 1) * gw])
        yield

        y_parts = []
        for g in range(GROUPS):
            lhs_p, rhs_p, decays = [], [], []
            for p in range(HEADS_PER_GROUP // 2):
                scores, b_scaled, e_acs = [], [], []
                for q in range(2):
                    hd = g * HEADS_PER_GROUP + 2 * p + q
                    acs_b = jnp.broadcast_to(a_cs[:, hd:hd + 1], (CHUNK, LANES))
                    seg = acs_b - tr[hd:hd + 1, :]
                    scores.append(
                        (cb[g] * jnp.exp2(jnp.where(causal, seg, -jnp.inf))).astype(BF16))
                    b_scaled.append((b_t[g] * tr[HEADS + hd:HEADS + hd + 1, :]).astype(BF16))
                    e_acs.append(jnp.exp2(acs_b))
                tcol = g * (HEADS_PER_GROUP // 2) + p
                rhs_p.append(pair_rhs(xc_ref[rows, tcol * LANES:(tcol + 1) * LANES]))
                lhs_p.append(jnp.concatenate([jnp.concatenate(scores, axis=1),
                                              jnp.concatenate(b_scaled, axis=1)], axis=0))
                decays.append(jnp.where(low_half, e_acs[0], e_acs[1]))
            res = jnp.dot(jnp.concatenate(lhs_p, axis=1), block_diag(rhs_p[0], rhs_p[1]),
                          preferred_element_type=F32)
            decay = jnp.concatenate(decays, axis=1)
            y_parts.append(res[0:CHUNK, :] + y_off[g] * decay)
            state_ref[g] = state[g] * decay[CHUNK - 1:CHUNK, :] + res[CHUNK:2 * CHUNK, :]

        for g in range(GROUPS):
            gcols = slice(g * GROUP_WIDTH, (g + 1) * GROUP_WIDTH)
            zs = proj_ref[rows, Z_COL + g * GROUP_WIDTH:Z_COL + (g + 1) * GROUP_WIDTH]
            yg = (y_parts[g] + dskip_ref[:, gcols] * xc_ref[rows, gcols]) * _silu(zs)
            yn = yg * lax.rsqrt(jnp.mean(yg * yg, axis=-1, keepdims=True) + EPS)
            mix_ref[rows, GM_WIDTH + g * GROUP_WIDTH:GM_WIDTH + (g + 1) * GROUP_WIDTH] = (
                yn * nrmw_ref[:, gcols]).astype(BF16)
        yield

    def mixers(r, fill):
        chains = [gmlp(r), ssd(r)]
        while chains:
            for chain in list(chains):
                try:
                    next(chain)
                except StopIteration:
                    chains.remove(chain)
                    continue
                fill()

    half_rows = BLOCK // 2
    cons = [dict() for _ in range(2)]

    def out_proj_piece(hf):
        hrows = slice(hf * half_rows, (hf + 1) * half_rows)
        o = jnp.dot(mix_ref[hrows, :], wout_ref[...], preferred_element_type=F32)
        x1 = xb_ref[hrows, :] + _rms(o, npost_ref[...])
        cons[hf]["x1"] = x1
        cons[hf]["h2"] = _rms(x1, npre_ref[...]).astype(BF16)

    ff_pieces = list(range(0, D_FF, FF_PIECE))
    full = {}

    def up_piece(f0):
        h2 = jnp.concatenate([cons[0]["h2"], cons[1]["h2"]], axis=0)
        up = jnp.dot(h2, wup_ref[:, f0:f0 + FF_PIECE], preferred_element_type=F32)
        full[f0] = jnp.square(jnp.maximum(up, 0.0)).astype(BF16)

    def down_piece(f0):
        dn = jnp.dot(full[f0], wdown_ref[f0:f0 + FF_PIECE, :], preferred_element_type=F32)
        full["acc"] = dn if "acc" not in full else full["acc"] + dn
        if f0 == ff_pieces[-1]:
            for hf in range(2):
                hrows = slice(hf * half_rows, (hf + 1) * half_rows)
                out_ref[hrows, :] = cons[hf]["x1"] + _rms(full["acc"][hrows, :], nffn_ref[...])

    out_proj_piece(0)
    queue = [functools.partial(out_proj_piece, 1)]
    mid = len(ff_pieces) // 2
    for part in (ff_pieces[:mid], ff_pieces[mid:]):
        queue += [functools.partial(up_piece, f0) for f0 in part]
        queue += [functools.partial(down_piece, f0) for f0 in part]
    per_slot = [5, 4, 4, 4]
    assert sum(per_slot) == len(queue)

    upper = slice(half_rows, BLOCK)
    proj_queue = (in_proj_pieces(xa_ref[upper, :], upper)
                  + in_proj_pieces(xn_ref[...], slice(0, half_rows)))
    proj_per_slot = len(proj_queue) // n_chunks

    for r in range(n_chunks):
        mine = [queue.pop(0) for _ in range(per_slot[r])]
        nxt = [proj_queue.pop(0) for _ in range(proj_per_slot)]
        fillers = []
        while mine or nxt:
            if mine:
                fillers.append(mine.pop(0))
            if nxt:
                fillers.append(nxt.pop(0))

        def fill(fillers=fillers):
            if fillers:
                fillers.pop(0)()

        mixers(r, fill)
        while fillers:
            fill()

    xpad_ref[0:PAD_ROWS, :] = xpad_ref[BLOCK:BLOCK + PAD_ROWS, :]


def _layer(x2d, bsz, seq, norm_mix_pre, w_in, gm_ln_w, gm_ln_b, gm_w_s, gm_b_s, conv_w, conv_b,
           dt_bias, a_log, d_skip, ssm_norm_w, w_out, norm_mix_post, norm_ffn_pre, w_up, w_down,
           norm_ffn_post):
    t = bsz * seq
    assert seq % BLOCK == 0, (seq, BLOCK)
    n_blocks = t // BLOCK
    row = lambda p: p.reshape(1, -1).astype(F32)

    w_main = w_in[:, :MAIN_COLS].astype(BF16)
    w_dt = jnp.pad(w_in[:, MAIN_COLS:], ((0, 0), (0, LANES - HEADS))).astype(BF16)
    causal = jnp.tril(jnp.ones((CHUNK, CHUNK), dtype=bool))
    w_s = jnp.where(causal[None], gm_w_s, 0.0).astype(BF16)
    wcat = w_s.reshape(HEADS // 4, 4, CHUNK, CHUNK).transpose(0, 2, 1, 3).reshape(
        HEADS // 4, CHUNK, 4 * CHUNK)
    bs_e = jnp.repeat(gm_b_s.T.astype(F32), HEAD_DIM, axis=1)
    dtb = jnp.pad(dt_bias.astype(F32), (0, LANES - HEADS)).reshape(1, LANES)
    a_pad = jnp.pad(-jnp.exp(a_log.astype(F32)), (0, LANES - HEADS)).reshape(1, LANES)
    dskip_e = jnp.repeat(d_skip.astype(F32), HEAD_DIM).reshape(1, SSM_WIDTH)
    head_of_col = jnp.arange(MEAN_WIDTH) // HEAD_DIM
    mavg = jnp.where(head_of_col[:, None] == head_of_col[None, :], 1.0 / HEAD_DIM, 0.0).astype(BF16)
    tri = causal.astype(BF16)

    def const(shape):
        zeros = (0,) * len(shape)
        return pl.BlockSpec(shape, lambda i: zeros, pipeline_mode=pl.Buffered(1))

    last = n_blocks - 1
    last_half = 2 * n_blocks - 1
    in_specs = [
        pl.BlockSpec((BLOCK, D_MODEL), lambda i: (jnp.minimum(i, last), 0)),
        pl.BlockSpec((BLOCK, D_MODEL), lambda i: (jnp.maximum(i - 1, 0), 0)),
        pl.BlockSpec((BLOCK // 2, D_MODEL),
                     lambda i: (jnp.minimum(2 * (i + 1), last_half), 0)),
        const((1, D_MODEL)),
        const((D_MODEL, MAIN_COLS)),
        const((D_MODEL, LANES)),
        const((1, GM_WIDTH)),
        const((1, GM_WIDTH)),
        const((HEADS // 4, CHUNK, 4 * CHUNK)),
        const((CHUNK, GM_WIDTH)),
        const((CONV, CONV_CH)),
        const((1, CONV_CH)),
        const((1, LANES)),
        const((1, LANES)),
        const((1, SSM_WIDTH)),
        const((1, SSM_WIDTH)),
        const((MEAN_WIDTH, MEAN_WIDTH)),
        const((CHUNK, CHUNK)),
        const((MIX_WIDTH, D_MODEL)),
        const((1, D_MODEL)),
        const((1, D_MODEL)),
        const((D_MODEL, D_FF)),
        const((D_FF, D_MODEL)),
        const((1, D_MODEL)),
    ]
    return pl.pallas_call(
        functools.partial(_layer_kernel, blocks_per_seq=seq // BLOCK, n_blocks=n_blocks),
        grid=(n_blocks + 1,),
        in_specs=in_specs,
        out_specs=pl.BlockSpec((BLOCK, D_MODEL), lambda i: (jnp.maximum(i - 1, 0), 0)),
        out_shape=jax.ShapeDtypeStruct((t, D_MODEL), F32),
        scratch_shapes=[
            pltpu.VMEM((BLOCK, MAIN_COLS), F32),
            pltpu.VMEM((BLOCK, LANES), F32),
            pltpu.VMEM((BLOCK, MIX_WIDTH), BF16),
            pltpu.VMEM((BLOCK + PAD_ROWS, CONV_CH), F32),
            pltpu.VMEM((GROUPS, STATE, GROUP_WIDTH), F32),
            pltpu.VMEM((BLOCK, CONV_CH), F32),
        ],
        compiler_params=pltpu.CompilerParams(
            dimension_semantics=("arbitrary",), vmem_limit_bytes=VMEM_LIMIT_BYTES),
        name="hybrid_layer",
    )(x2d, x2d, x2d, row(norm_mix_pre), w_main, w_dt, row(gm_ln_w), row(gm_ln_b), wcat, bs_e,
      conv_w.astype(F32), row(conv_b), dtb, a_pad, dskip_e, row(ssm_norm_w), mavg, tri,
      w_out.astype(BF16), row(norm_mix_post), row(norm_ffn_pre), w_up.astype(BF16),
      w_down.astype(BF16), row(norm_ffn_post))


def kernel(x, norm_mix_pre, w_in, gm_ln_w, gm_ln_b, gm_w_s, gm_b_s, conv_w, conv_b, dt_bias, a_log, d_skip, ssm_norm_w, w_out, norm_mix_post, norm_ffn_pre, w_up, w_down, norm_ffn_post):
    bsz, seq, d = x.shape
    depth = w_in.shape[0]
    x2d = x.reshape(bsz * seq, d)
    for i in range(depth):
        x2d = _layer(x2d, bsz, seq, norm_mix_pre[i], w_in[i], gm_ln_w[i], gm_ln_b[i], gm_w_s[i],
                     gm_b_s[i], conv_w[i], conv_b[i], dt_bias[i], a_log[i], d_skip[i],
                     ssm_norm_w[i], w_out[i], norm_mix_post[i], norm_ffn_pre[i], w_up[i],
                     w_down[i], norm_ffn_post[i])
    return x2d.reshape(bsz, seq, d)
```

```python
import functools
import math

import jax
import jax.numpy as jnp
from jax import lax
from jax.experimental import pallas as pl
from jax.experimental.pallas import tpu as pltpu

F32 = jnp.float32
BF16 = jnp.bfloat16

D_MODEL = 1024
GM_WIDTH = 512
HEAD_DIM = 64
HEADS = 8
CHUNK = 128
SSM_WIDTH = 512
GROUPS = 2
HEADS_PER_GROUP = HEADS // GROUPS
GROUP_WIDTH = HEADS_PER_GROUP * HEAD_DIM
STATE = 128
CONV = 4
CONV_CH = SSM_WIDTH + 2 * GROUPS * STATE
D_FF = 4 * D_MODEL
EPS = 1e-6
MAIN_COLS = 2 * GM_WIDTH + SSM_WIDTH + CONV_CH
Z_COL = 2 * GM_WIDTH
XBC_COL = Z_COL + SSM_WIDTH
MIX_WIDTH = GM_WIDTH + SSM_WIDTH
LANES = 128
MEAN_WIDTH = 256
GMLP_CHUNKS = 2
PAD_ROWS = 8
LOG2E = 1.4426950408889634

VMEM_LIMIT_BYTES = 60 * 1024 * 1024

BLOCK = 512
PROJ_BLOCK = 512
FF_PIECE = 512


def _rms(x, w):
    return x * lax.rsqrt(jnp.mean(x * x, axis=-1, keepdims=True) + EPS) * w


def _split_dot_left(m, a, passes):
    pieces = []
    rem = a
    for _ in range(passes):
        piece = rem.astype(BF16)
        pieces.append(piece)
        rem = rem - piece.astype(F32)
    d = jnp.dot(m, jnp.concatenate(pieces, axis=1), preferred_element_type=F32)
    width = a.shape[1]
    acc = d[:, 0:width]
    for k in range(1, passes):
        acc = acc + d[:, k * width:(k + 1) * width]
    return acc


_GELU_A = -2.0 * math.sqrt(2.0 / math.pi) * LOG2E
_GELU_B = _GELU_A * 0.044715


def _gelu(x):
    return x / (1.0 + jnp.exp2(x * (_GELU_A + _GELU_B * (x * x))))


def _silu(x):
    return x / (1.0 + jnp.exp2(x * (-LOG2E)))


def _softplus(x):
    return jnp.maximum(x, 0.0) + jnp.log1p(jnp.exp(-jnp.abs(x)))


def _layer_kernel(
        xa_ref, xb_ref, xn_ref, nmix_ref, win_ref, wdt_ref, lnw_ref, lnb_ref, wcat_ref, bs_ref,
        convw_ref, convb_ref, dtb_ref, a_ref, dskip_ref, nrmw_ref, mavg_ref, tri_ref,
        wout_ref, npost_ref, npre_ref, wup_ref, wdown_ref, nffn_ref,
        out_ref,
        proj_ref, dtraw_ref, mix_ref, xpad_ref, state_ref, xc_ref,
        *, blocks_per_seq, n_blocks):
    i = pl.program_id(0)
    n_chunks = BLOCK // CHUNK
    first_of_seq = lax.rem(jnp.minimum(i, n_blocks - 1), blocks_per_seq) == 0

    def in_proj_pieces(xv, dst):
        hn = _rms(xv, nmix_ref[...]).astype(BF16)

        def main(n0):
            proj_ref[dst, n0:n0 + PROJ_BLOCK] = jnp.dot(
                hn, win_ref[:, n0:n0 + PROJ_BLOCK], preferred_element_type=F32)

        def dt():
            dtraw_ref[dst, :] = jnp.dot(hn, wdt_ref[...], preferred_element_type=F32)

        return [functools.partial(main, n0) for n0 in range(0, MAIN_COLS, PROJ_BLOCK)] + [dt]

    @pl.when(i == 0)
    def _():
        mix_ref[...] = jnp.zeros(mix_ref.shape, BF16)
        for piece in in_proj_pieces(xa_ref[0:BLOCK // 2, :], slice(0, BLOCK // 2)):
            piece()

    @pl.when(first_of_seq)
    def _():
        xpad_ref[0:PAD_ROWS, :] = jnp.zeros((PAD_ROWS, CONV_CH), F32)
        state_ref[...] = jnp.zeros(state_ref.shape, F32)

    lane = lax.broadcasted_iota(jnp.int32, (CHUNK, LANES), 1)
    row = lax.broadcasted_iota(jnp.int32, (CHUNK, LANES), 0)
    causal = row >= lane
    low_half = lane < HEAD_DIM

    def pair_rhs(tile):
        return jnp.concatenate([jnp.where(low_half, tile, 0.0).astype(BF16),
                                jnp.where(low_half, 0.0, tile).astype(BF16)], axis=0)

    def gmlp(first, count):
        nrows = count * CHUNK
        rows = slice(first * CHUNK, first * CHUNK + nrows)

        def head_mean(a):
            ab = a.astype(BF16)
            col_blocks = range(0, GM_WIDTH, MEAN_WIDTH)
            stacked = jnp.concatenate([ab[:, c0:c0 + MEAN_WIDTH] for c0 in col_blocks], axis=0)
            m = jnp.dot(stacked, mavg_ref[...], preferred_element_type=F32)
            return jnp.concatenate(
                [m[k * nrows:(k + 1) * nrows, :] for k in range(len(col_blocks))], axis=1)

        v = _gelu(proj_ref[rows, GM_WIDTH:2 * GM_WIDTH])
        mu = head_mean(v)
        yield
        d = v - mu
        var = head_mean(d * d)
        yield
        vn = d * lax.rsqrt(var + EPS) * lnw_ref[...] + lnb_ref[...]
        for k in range(GM_WIDTH // LANES):
            cols = slice(k * LANES, (k + 1) * LANES)
            rhs = jnp.concatenate(
                [pair_rhs(vn[c * CHUNK:(c + 1) * CHUNK, cols]) for c in range(count)], axis=1)
            mixed = jnp.dot(wcat_ref[k], rhs, preferred_element_type=F32)
            for c in range(count):
                crows = slice((first + c) * CHUNK, (first + c + 1) * CHUNK)
                gate = mixed[:, c * LANES:(c + 1) * LANES] + bs_ref[:, cols]
                mix_ref[crows, cols] = (_gelu(proj_ref[crows, cols]) * gate).astype(BF16)
        yield

    def ssd(r):
        r0 = r * CHUNK
        rows = slice(r0, r0 + CHUNK)
        prows = slice(r0 + PAD_ROWS, r0 + PAD_ROWS + CHUNK)

        dt = _softplus(dtraw_ref[rows, :] + dtb_ref[...])
        dta = dt * (a_ref[...] * LOG2E)
        a_cs = _split_dot_left(tri_ref[...], dta, 3)
        yield

        xpad_ref[prows, :] = proj_ref[rows, XBC_COL:MAIN_COLS]
        for ct in range(CONV_CH // LANES):
            cols = slice(ct * LANES, (ct + 1) * LANES)
            acc = convb_ref[:, cols] + convw_ref[CONV - 1:CONV, cols] * xpad_ref[prows, cols]
            for k in range(1, CONV):
                acc = acc + convw_ref[CONV - 1 - k:CONV - k, cols] * xpad_ref[
                    r0 + PAD_ROWS - k:r0 + PAD_ROWS - k + CHUNK, cols]
            xc_ref[rows, cols] = _silu(acc)

        total = a_cs[CHUNK - 1:CHUNK, :]
        w_end = dt * jnp.exp2(total - a_cs)
        tr = jnp.where(lane < HEADS, a_cs - jnp.log(dt) * LOG2E, pltpu.roll(w_end, HEADS, 1)).T

        b_t, cb, y_off, state = [], [], [], []
        for g in range(GROUPS):
            b_g = xc_ref[rows, SSM_WIDTH + g * STATE:SSM_WIDTH + (g + 1) * STATE]
            c_g = xc_ref[rows, SSM_WIDTH + (GROUPS + g) * STATE:
                         SSM_WIDTH + (GROUPS + g + 1) * STATE].astype(BF16)
            cb.append(lax.dot_general(c_g, b_g.astype(BF16), (((1,), (1,)), ((), ())),
                                      preferred_element_type=F32))
            state.append(state_ref[g])
            y_off.append(jnp.dot(c_g, state[g].astype(BF16), preferred_element_type=F32))
            b_t.append(b_g.T)
        yield

        y_parts = []
        for g in range(GROUPS):
            y_tiles = []
            for p in range(HEADS_PER_GROUP // 2):
                pcols = slice(p * LANES, (p + 1) * LANES)
                scores, b_scaled, e_acs = [], [], []
                for q in range(2):
                    hd = g * HEADS_PER_GROUP + 2 * p + q
                    acs_b = jnp.broadcast_to(a_cs[:, hd:hd + 1], (CHUNK, LANES))
                    seg = acs_b - tr[hd:hd + 1, :]
                    scores.append(
                        (cb[g] * jnp.exp2(jnp.where(causal, seg, -jnp.inf))).astype(BF16))
                    b_scaled.append((b_t[g] * tr[HEADS + hd:HEADS + hd + 1, :]).astype(BF16))
                    e_acs.append(jnp.exp2(acs_b))
                tcol = g * (HEADS_PER_GROUP // 2) + p
                rhs = pair_rhs(xc_ref[rows, tcol * LANES:(tcol + 1) * LANES])
                lhs = jnp.concatenate([jnp.concatenate(scores, axis=1),
                                       jnp.concatenate(b_scaled, axis=1)], axis=0)
                res = jnp.dot(lhs, rhs, preferred_element_type=F32)
                decay = jnp.where(low_half, e_acs[0], e_acs[1])
                y_tiles.append(res[0:CHUNK, :] + y_off[g][:, pcols] * decay)
                state_ref[g, :, pcols] = (state[g][:, pcols] * decay[CHUNK - 1:CHUNK, :]
                                          + res[CHUNK:2 * CHUNK, :])
            y_parts.append(jnp.concatenate(y_tiles, axis=1))

        for g in range(GROUPS):
            gcols = slice(g * GROUP_WIDTH, (g + 1) * GROUP_WIDTH)
            zs = proj_ref[rows, Z_COL + g * GROUP_WIDTH:Z_COL + (g + 1) * GROUP_WIDTH]
            yg = (y_parts[g] + dskip_ref[:, gcols] * xc_ref[rows, gcols]) * _silu(zs)
            yn = yg * lax.rsqrt(jnp.mean(yg * yg, axis=-1, keepdims=True) + EPS)
            mix_ref[rows, GM_WIDTH + g * GROUP_WIDTH:GM_WIDTH + (g + 1) * GROUP_WIDTH] = (
                yn * nrmw_ref[:, gcols]).astype(BF16)
        yield

    def mixers(r, fill):
        chains = [ssd(r)]
        if r % GMLP_CHUNKS == 0:
            chains.insert(0, gmlp(r, GMLP_CHUNKS))
        while chains:
            for chain in list(chains):
                try:
                    next(chain)
                except StopIteration:
                    chains.remove(chain)
                    continue
                fill()
                if len(chains) == 1:
                    fill()

    half_rows = BLOCK // 2
    cons = [dict() for _ in range(2)]

    def out_proj_piece(hf):
        hrows = slice(hf * half_rows, (hf + 1) * half_rows)
        o = jnp.dot(mix_ref[hrows, :], wout_ref[...], preferred_element_type=F32)
        x1 = xb_ref[hrows, :] + _rms(o, npost_ref[...])
        cons[hf]["x1"] = x1
        cons[hf]["h2"] = _rms(x1, npre_ref[...]).astype(BF16)

    ff_pieces = list(range(0, D_FF, FF_PIECE))
    full = {}

    def up_piece(f0):
        h2 = jnp.concatenate([cons[0]["h2"], cons[1]["h2"]], axis=0)
        up = jnp.dot(h2, wup_ref[:, f0:f0 + FF_PIECE], preferred_element_type=F32)
        full[f0] = jnp.square(jnp.maximum(up, 0.0)).astype(BF16)

    def down_piece(f0):
        dn = jnp.dot(full[f0], wdown_ref[f0:f0 + FF_PIECE, :], preferred_element_type=F32)
        full["acc"] = dn if "acc" not in full else full["acc"] + dn
        if f0 == ff_pieces[-1]:
            for hf in range(2):
                hrows = slice(hf * half_rows, (hf + 1) * half_rows)
                out_ref[hrows, :] = cons[hf]["x1"] + _rms(full["acc"][hrows, :], nffn_ref[...])

    out_proj_piece(0)
    queue = [functools.partial(out_proj_piece, 1)]
    mid = len(ff_pieces) // 2
    for part in (ff_pieces[:mid], ff_pieces[mid:]):
        queue += [functools.partial(up_piece, f0) for f0 in part]
        queue += [functools.partial(down_piece, f0) for f0 in part]
    per_slot = [5, 4, 4, 4]
    assert sum(per_slot) == len(queue)

    upper = slice(half_rows, BLOCK)
    proj_queue = (in_proj_pieces(xa_ref[upper, :], upper)
                  + in_proj_pieces(xn_ref[...], slice(0, half_rows)))
    proj_per_slot = len(proj_queue) // n_chunks

    for r in range(n_chunks):
        mine = [queue.pop(0) for _ in range(per_slot[r])]
        nxt = [proj_queue.pop(0) for _ in range(proj_per_slot)]
        fillers = []
        while mine or nxt:
            if mine:
                fillers.append(mine.pop(0))
            if nxt:
                fillers.append(nxt.pop(0))

        def fill(fillers=fillers):
            if fillers:
                fillers.pop(0)()

        mixers(r, fill)
        while fillers:
            fill()

    xpad_ref[0:PAD_ROWS, :] = xpad_ref[BLOCK:BLOCK + PAD_ROWS, :]


def _layer(x2d, bsz, seq, norm_mix_pre, w_in, gm_ln_w, gm_ln_b, gm_w_s, gm_b_s, conv_w, conv_b,
           dt_bias, a_log, d_skip, ssm_norm_w, w_out, norm_mix_post, norm_ffn_pre, w_up, w_down,
           norm_ffn_post):
    t = bsz * seq
    assert seq % BLOCK == 0, (seq, BLOCK)
    n_blocks = t // BLOCK
    row = lambda p: p.reshape(1, -1).astype(F32)

    w_main = w_in[:, :MAIN_COLS].astype(BF16)
    w_dt = jnp.pad(w_in[:, MAIN_COLS:], ((0, 0), (0, LANES - HEADS))).astype(BF16)
    causal = jnp.tril(jnp.ones((CHUNK, CHUNK), dtype=bool))
    w_s = jnp.where(causal[None], gm_w_s, 0.0).astype(BF16)
    wcat = w_s.reshape(HEADS // 2, 2, CHUNK, CHUNK).transpose(0, 2, 1, 3).reshape(
        HEADS // 2, CHUNK, 2 * CHUNK)
    bs_e = jnp.repeat(gm_b_s.T.astype(F32), HEAD_DIM, axis=1)
    dtb = jnp.pad(dt_bias.astype(F32), (0, LANES - HEADS)).reshape(1, LANES)
    a_pad = jnp.pad(-jnp.exp(a_log.astype(F32)), (0, LANES - HEADS)).reshape(1, LANES)
    dskip_e = jnp.repeat(d_skip.astype(F32), HEAD_DIM).reshape(1, SSM_WIDTH)
    head_of_col = jnp.arange(MEAN_WIDTH) // HEAD_DIM
    mavg = jnp.where(head_of_col[:, None] == head_of_col[None, :], 1.0 / HEAD_DIM, 0.0).astype(BF16)
    tri = causal.astype(BF16)

    def const(shape):
        zeros = (0,) * len(shape)
        return pl.BlockSpec(shape, lambda i: zeros, pipeline_mode=pl.Buffered(1))

    last = n_blocks - 1
    last_half = 2 * n_blocks - 1
    in_specs = [
        pl.BlockSpec((BLOCK, D_MODEL), lambda i: (jnp.minimum(i, last), 0)),
        pl.BlockSpec((BLOCK, D_MODEL), lambda i: (jnp.maximum(i - 1, 0), 0)),
        pl.BlockSpec((BLOCK // 2, D_MODEL),
                     lambda i: (jnp.minimum(2 * (i + 1), last_half), 0)),
        const((1, D_MODEL)),
        const((D_MODEL, MAIN_COLS)),
        const((D_MODEL, LANES)),
        const((1, GM_WIDTH)),
        const((1, GM_WIDTH)),
        const((HEADS // 2, CHUNK, 2 * CHUNK)),
        const((CHUNK, GM_WIDTH)),
        const((CONV, CONV_CH)),
        const((1, CONV_CH)),
        const((1, LANES)),
        const((1, LANES)),
        const((1, SSM_WIDTH)),
        const((1, SSM_WIDTH)),
        const((MEAN_WIDTH, MEAN_WIDTH)),
        const((CHUNK, CHUNK)),
        const((MIX_WIDTH, D_MODEL)),
        const((1, D_MODEL)),
        const((1, D_MODEL)),
        const((D_MODEL, D_FF)),
        const((D_FF, D_MODEL)),
        const((1, D_MODEL)),
    ]
    return pl.pallas_call(
        functools.partial(_layer_kernel, blocks_per_seq=seq // BLOCK, n_blocks=n_blocks),
        grid=(n_blocks + 1,),
        in_specs=in_specs,
        out_specs=pl.BlockSpec((BLOCK, D_MODEL), lambda i: (jnp.maximum(i - 1, 0), 0)),
        out_shape=jax.ShapeDtypeStruct((t, D_MODEL), F32),
        scratch_shapes=[
            pltpu.VMEM((BLOCK, MAIN_COLS), F32),
            pltpu.VMEM((BLOCK, LANES), F32),
            pltpu.VMEM((BLOCK, MIX_WIDTH), BF16),
            pltpu.VMEM((BLOCK + PAD_ROWS, CONV_CH), F32),
            pltpu.VMEM((GROUPS, STATE, GROUP_WIDTH), F32),
            pltpu.VMEM((BLOCK, CONV_CH), F32),
        ],
        compiler_params=pltpu.CompilerParams(
            dimension_semantics=("arbitrary",), vmem_limit_bytes=VMEM_LIMIT_BYTES),
        name="hybrid_layer",
    )(x2d, x2d, x2d, row(norm_mix_pre), w_main, w_dt, row(gm_ln_w), row(gm_ln_b), wcat, bs_e,
      conv_w.astype(F32), row(conv_b), dtb, a_pad, dskip_e, row(ssm_norm_w), mavg, tri,
      w_out.astype(BF16), row(norm_mix_post), row(norm_ffn_pre), w_up.astype(BF16),
      w_down.astype(BF16), row(norm_ffn_post))


def kernel(x, norm_mix_pre, w_in, gm_ln_w, gm_ln_b, gm_w_s, gm_b_s, conv_w, conv_b, dt_bias, a_log, d_skip, ssm_norm_w, w_out, norm_mix_post, norm_ffn_pre, w_up, w_down, norm_ffn_post):
    bsz, seq, d = x.shape
    depth = w_in.shape[0]
    x2d = x.reshape(bsz * seq, d)
    for i in range(depth):
        x2d = _layer(x2d, bsz, seq, norm_mix_pre[i], w_in[i], gm_ln_w[i], gm_ln_b[i], gm_w_s[i],
                     gm_b_s[i], conv_w[i], conv_b[i], dt_bias[i], a_log[i], d_skip[i],
                     ssm_norm_w[i], w_out[i], norm_mix_post[i], norm_ffn_pre[i], w_up[i],
                     w_down[i], norm_ffn_post[i])
    return x2d.reshape(bsz, seq, d)
```

```python
import functools
import math

import jax
import jax.numpy as jnp
from jax import lax
from jax.experimental import pallas as pl
from jax.experimental.pallas import tpu as pltpu

F32 = jnp.float32
BF16 = jnp.bfloat16

D_MODEL = 1024
GM_WIDTH = 512
HEAD_DIM = 64
HEADS = 8
CHUNK = 128
SSM_WIDTH = 512
GROUPS = 2
HEADS_PER_GROUP = HEADS // GROUPS
GROUP_WIDTH = HEADS_PER_GROUP * HEAD_DIM
STATE = 128
CONV = 4
CONV_CH = SSM_WIDTH + 2 * GROUPS * STATE
D_FF = 4 * D_MODEL
EPS = 1e-6
MAIN_COLS = 2 * GM_WIDTH + SSM_WIDTH + CONV_CH
Z_COL = 2 * GM_WIDTH
XBC_COL = Z_COL + SSM_WIDTH
MIX_WIDTH = GM_WIDTH + SSM_WIDTH
LANES = 128
MEAN_WIDTH = 256
PAD_ROWS = 8
W_IN_COLS = MAIN_COLS + LANES
ROW_NORM_MIX_PRE, ROW_NORM_MIX_POST, ROW_NORM_FFN_PRE, ROW_NORM_FFN_POST = 0, 1, 2, 3
ROW_CONV_B = 4
ROW_CONV_W = 5
ROW_GM_LN = ROW_CONV_W + CONV
ROW_SSM = ROW_GM_LN + 1
ROW_DT = ROW_SSM + 1
VEC_ROWS = 16
LOG2E = 1.4426950408889634

VMEM_LIMIT_BYTES = 60 * 1024 * 1024

BLOCK = 512
PROJ_BLOCK = 512
FF_PIECE = 512


def _rms(x, w):
    return x * lax.rsqrt(jnp.mean(x * x, axis=-1, keepdims=True) + EPS) * w


def _split_dot_left(m, a, passes):
    pieces = []
    rem = a
    for _ in range(passes):
        piece = rem.astype(BF16)
        pieces.append(piece)
        rem = rem - piece.astype(F32)
    d = jnp.dot(m, jnp.concatenate(pieces, axis=1), preferred_element_type=F32)
    width = a.shape[1]
    acc = d[:, 0:width]
    for k in range(1, passes):
        acc = acc + d[:, k * width:(k + 1) * width]
    return acc


_GELU_A = -2.0 * math.sqrt(2.0 / math.pi) * LOG2E
_GELU_B = _GELU_A * 0.044715


def _gelu(x):
    return x / (1.0 + jnp.exp2(x * (_GELU_A + _GELU_B * (x * x))))


def _silu(x):
    return x / (1.0 + jnp.exp2(x * (-LOG2E)))


def _softplus(x):
    return jnp.maximum(x, 0.0) + jnp.log1p(jnp.exp(-jnp.abs(x)))


def _layer_kernel(
        xa_ref, xb_ref, xn_ref, vec_ref, win_ref, wcat_ref, bs_ref, mavg_ref, tri_ref,
        wout_ref, wup_ref, wdown_ref,
        out_ref,
        proj_ref, dtraw_ref, mix_ref, xpad_ref, state_ref, xc_ref,
        *, blocks_per_seq, n_blocks):
    i = pl.program_id(0)
    n_chunks = BLOCK // CHUNK
    first_of_seq = lax.rem(jnp.minimum(i, n_blocks - 1), blocks_per_seq) == 0

    def in_proj_pieces(xv, dst):
        hn = _rms(xv, vec_ref[ROW_NORM_MIX_PRE:ROW_NORM_MIX_PRE + 1, :]).astype(BF16)

        def main(n0):
            proj_ref[dst, n0:n0 + PROJ_BLOCK] = jnp.dot(
                hn, win_ref[:, n0:n0 + PROJ_BLOCK], preferred_element_type=F32)

        def dt():
            dtraw_ref[dst, :] = jnp.dot(hn, win_ref[:, MAIN_COLS:MAIN_COLS + LANES],
                                        preferred_element_type=F32)

        return [functools.partial(main, n0) for n0 in range(0, MAIN_COLS, PROJ_BLOCK)] + [dt]

    @pl.when(i == 0)
    def _():
        mix_ref[...] = jnp.zeros(mix_ref.shape, BF16)
        for piece in in_proj_pieces(xa_ref[0:BLOCK // 2, :], slice(0, BLOCK // 2)):
            piece()

    @pl.when(first_of_seq)
    def _():
        xpad_ref[0:PAD_ROWS, :] = jnp.zeros((PAD_ROWS, CONV_CH), F32)
        state_ref[...] = jnp.zeros(state_ref.shape, F32)

    lane = lax.broadcasted_iota(jnp.int32, (CHUNK, LANES), 1)
    row = lax.broadcasted_iota(jnp.int32, (CHUNK, LANES), 0)
    causal = row >= lane
    low_half = lane < HEAD_DIM

    def pair_rhs(tile):
        return jnp.concatenate([jnp.where(low_half, tile, 0.0).astype(BF16),
                                jnp.where(low_half, 0.0, tile).astype(BF16)], axis=0)

    def gmlp(r):
        rows = slice(r * CHUNK, (r + 1) * CHUNK)

        def head_mean(a):
            ab = a.astype(BF16)
            col_blocks = range(0, GM_WIDTH, MEAN_WIDTH)
            stacked = jnp.concatenate([ab[:, c0:c0 + MEAN_WIDTH] for c0 in col_blocks], axis=0)
            m = jnp.dot(stacked, mavg_ref[...], preferred_element_type=F32)
            return jnp.concatenate(
                [m[k * CHUNK:(k + 1) * CHUNK, :] for k in range(len(col_blocks))], axis=1)

        v = _gelu(proj_ref[rows, GM_WIDTH:2 * GM_WIDTH])
        mu = head_mean(v)
        yield
        d = v - mu
        var = head_mean(d * d)
        yield
        vn = (d * lax.rsqrt(var + EPS) * vec_ref[ROW_GM_LN:ROW_GM_LN + 1, 0:GM_WIDTH]
              + vec_ref[ROW_GM_LN:ROW_GM_LN + 1, GM_WIDTH:2 * GM_WIDTH])
        for k in range(GM_WIDTH // LANES):
            cols = slice(k * LANES, (k + 1) * LANES)
            mixed = jnp.dot(wcat_ref[k], pair_rhs(vn[:, cols]),
                            preferred_element_type=F32) + bs_ref[:, cols]
            mix_ref[rows, cols] = (_gelu(proj_ref[rows, cols]) * mixed).astype(BF16)
        yield

    def ssd(r):
        r0 = r * CHUNK
        rows = slice(r0, r0 + CHUNK)
        prows = slice(r0 + PAD_ROWS, r0 + PAD_ROWS + CHUNK)

        dt = _softplus(dtraw_ref[rows, :] + vec_ref[ROW_DT:ROW_DT + 1, 0:LANES])
        dta = dt * (vec_ref[ROW_DT:ROW_DT + 1, LANES:2 * LANES] * LOG2E)
        a_cs = _split_dot_left(tri_ref[...], dta, 3)
        yield

        xpad_ref[prows, :] = proj_ref[rows, XBC_COL:MAIN_COLS]
        for ct in range(CONV_CH // LANES):
            cols = slice(ct * LANES, (ct + 1) * LANES)
            acc = (vec_ref[ROW_CONV_B:ROW_CONV_B + 1, cols]
                   + vec_ref[ROW_CONV_W + CONV - 1:ROW_CONV_W + CONV, cols] * xpad_ref[prows, cols])
            for k in range(1, CONV):
                acc = acc + vec_ref[ROW_CONV_W + CONV - 1 - k:ROW_CONV_W + CONV - k, cols] * xpad_ref[
                    r0 + PAD_ROWS - k:r0 + PAD_ROWS - k + CHUNK, cols]
            xc_ref[rows, cols] = _silu(acc)

        total = a_cs[CHUNK - 1:CHUNK, :]
        w_end = dt * jnp.exp2(total - a_cs)
        tr = jnp.where(lane < HEADS, a_cs - jnp.log(dt) * LOG2E, pltpu.roll(w_end, HEADS, 1)).T

        b_t, cb, y_off, state = [], [], [], []
        for g in range(GROUPS):
            b_g = xc_ref[rows, SSM_WIDTH + g * STATE:SSM_WIDTH + (g + 1) * STATE]
            c_g = xc_ref[rows, SSM_WIDTH + (GROUPS + g) * STATE:
                         SSM_WIDTH + (GROUPS + g + 1) * STATE].astype(BF16)
            cb.append(lax.dot_general(c_g, b_g.astype(BF16), (((1,), (1,)), ((), ())),
                                      preferred_element_type=F32))
            state.append(state_ref[g])
            y_off.append(jnp.dot(c_g, state[g].astype(BF16), preferred_element_type=F32))
            b_t.append(b_g.T)
        yield

        y_parts = []
        for g in range(GROUPS):
            y_tiles = []
            for p in range(HEADS_PER_GROUP // 2):
                pcols = slice(p * LANES, (p + 1) * LANES)
                scores, b_scaled, e_acs = [], [], []
                for q in range(2):
                    hd = g * HEADS_PER_GROUP + 2 * p + q
                    acs_b = jnp.broadcast_to(a_cs[:, hd:hd + 1], (CHUNK, LANES))
                    seg = acs_b - tr[hd:hd + 1, :]
                    scores.append(
                        (cb[g] * jnp.exp2(jnp.where(causal, seg, -jnp.inf))).astype(BF16))
                    b_scaled.append((b_t[g] * tr[HEADS + hd:HEADS + hd + 1, :]).astype(BF16))
                    e_acs.append(jnp.exp2(acs_b))
                tcol = g * (HEADS_PER_GROUP // 2) + p
                rhs = pair_rhs(xc_ref[rows, tcol * LANES:(tcol + 1) * LANES])
                lhs = jnp.concatenate([jnp.concatenate(scores, axis=1),
                                       jnp.concatenate(b_scaled, axis=1)], axis=0)
                res = jnp.dot(lhs, rhs, preferred_element_type=F32)
                decay = jnp.where(low_half, e_acs[0], e_acs[1])
                y_tiles.append(res[0:CHUNK, :] + y_off[g][:, pcols] * decay)
                state_ref[g, :, pcols] = (state[g][:, pcols] * decay[CHUNK - 1:CHUNK, :]
                                          + res[CHUNK:2 * CHUNK, :])
            y_parts.append(jnp.concatenate(y_tiles, axis=1))

        for g in range(GROUPS):
            gcols = slice(g * GROUP_WIDTH, (g + 1) * GROUP_WIDTH)
            zs = proj_ref[rows, Z_COL + g * GROUP_WIDTH:Z_COL + (g + 1) * GROUP_WIDTH]
            ncols = slice(SSM_WIDTH + g * GROUP_WIDTH, SSM_WIDTH + (g + 1) * GROUP_WIDTH)
            yg = (y_parts[g] + vec_ref[ROW_SSM:ROW_SSM + 1, gcols] * xc_ref[rows, gcols]) * _silu(zs)
            yn = yg * lax.rsqrt(jnp.mean(yg * yg, axis=-1, keepdims=True) + EPS)
            mix_ref[rows, GM_WIDTH + g * GROUP_WIDTH:GM_WIDTH + (g + 1) * GROUP_WIDTH] = (
                yn * vec_ref[ROW_SSM:ROW_SSM + 1, ncols]).astype(BF16)
        yield

    def mixers(r, fill):
        chains = [gmlp(r), ssd(r)]
        while chains:
            for chain in list(chains):
                try:
                    next(chain)
                except StopIteration:
                    chains.remove(chain)
                    continue
                fill()

    half_rows = BLOCK // 2
    cons = [dict() for _ in range(2)]

    def out_proj_piece(hf):
        hrows = slice(hf * half_rows, (hf + 1) * half_rows)
        o = jnp.dot(mix_ref[hrows, :], wout_ref[...], preferred_element_type=F32)
        x1 = xb_ref[hrows, :] + _rms(o, vec_ref[ROW_NORM_MIX_POST:ROW_NORM_MIX_POST + 1, :])
        cons[hf]["x1"] = x1
        cons[hf]["h2"] = _rms(x1, vec_ref[ROW_NORM_FFN_PRE:ROW_NORM_FFN_PRE + 1, :]).astype(BF16)

    ff_pieces = list(range(0, D_FF, FF_PIECE))
    full = {}

    def up_piece(f0):
        h2 = jnp.concatenate([cons[0]["h2"], cons[1]["h2"]], axis=0)
        up = jnp.dot(h2, wup_ref[:, f0:f0 + FF_PIECE], preferred_element_type=F32)
        full[f0] = jnp.square(jnp.maximum(up, 0.0)).astype(BF16)

    def down_piece(f0):
        dn = jnp.dot(full[f0], wdown_ref[f0:f0 + FF_PIECE, :], preferred_element_type=F32)
        full["acc"] = dn if "acc" not in full else full["acc"] + dn
        if f0 == ff_pieces[-1]:
            for hf in range(2):
                hrows = slice(hf * half_rows, (hf + 1) * half_rows)
                out_ref[hrows, :] = cons[hf]["x1"] + _rms(
                    full["acc"][hrows, :], vec_ref[ROW_NORM_FFN_POST:ROW_NORM_FFN_POST + 1, :])

    out_proj_piece(0)
    queue = [functools.partial(out_proj_piece, 1)]
    mid = len(ff_pieces) // 2
    for part in (ff_pieces[:mid], ff_pieces[mid:]):
        queue += [functools.partial(up_piece, f0) for f0 in part]
        queue += [functools.partial(down_piece, f0) for f0 in part]
    per_slot = [5, 4, 4, 4]
    assert sum(per_slot) == len(queue)

    upper = slice(half_rows, BLOCK)
    proj_queue = (in_proj_pieces(xa_ref[upper, :], upper)
                  + in_proj_pieces(xn_ref[...], slice(0, half_rows)))
    proj_per_slot = len(proj_queue) // n_chunks

    for r in range(n_chunks):
        mine = [queue.pop(0) for _ in range(per_slot[r])]
        nxt = [proj_queue.pop(0) for _ in range(proj_per_slot)]
        fillers = []
        while mine or nxt:
            if mine:
                fillers.append(mine.pop(0))
            if nxt:
                fillers.append(nxt.pop(0))

        def fill(fillers=fillers):
            if fillers:
                fillers.pop(0)()

        mixers(r, fill)
        while fillers:
            fill()

    xpad_ref[0:PAD_ROWS, :] = xpad_ref[BLOCK:BLOCK + PAD_ROWS, :]


def _layer(x2d, bsz, seq, norm_mix_pre, w_in, gm_ln_w, gm_ln_b, gm_w_s, gm_b_s, conv_w, conv_b,
           dt_bias, a_log, d_skip, ssm_norm_w, w_out, norm_mix_post, norm_ffn_pre, w_up, w_down,
           norm_ffn_post):
    t = bsz * seq
    assert seq % BLOCK == 0, (seq, BLOCK)
    n_blocks = t // BLOCK

    w_all = jnp.pad(w_in, ((0, 0), (0, W_IN_COLS - w_in.shape[1]))).astype(BF16)
    causal = jnp.tril(jnp.ones((CHUNK, CHUNK), dtype=bool))
    w_s = jnp.where(causal[None], gm_w_s, 0.0).astype(BF16)
    wcat = w_s.reshape(HEADS // 2, 2, CHUNK, CHUNK).transpose(0, 2, 1, 3).reshape(
        HEADS // 2, CHUNK, 2 * CHUNK)
    bs_e = jnp.repeat(gm_b_s.T.astype(F32), HEAD_DIM, axis=1)
    lane_pad = lambda p: jnp.pad(p.astype(F32), (0, LANES - HEADS))
    dt_row = jnp.concatenate([lane_pad(dt_bias), lane_pad(-jnp.exp(a_log.astype(F32))),
                              jnp.zeros((D_MODEL - 2 * LANES,), F32)])
    vec_rows = [norm_mix_pre, norm_mix_post, norm_ffn_pre, norm_ffn_post, conv_b,
                *[conv_w[k] for k in range(CONV)],
                jnp.concatenate([gm_ln_w.reshape(-1), gm_ln_b.reshape(-1)]),
                jnp.concatenate([jnp.repeat(d_skip, HEAD_DIM), ssm_norm_w]),
                dt_row]
    assert len(vec_rows) == ROW_DT + 1
    vecs = jnp.stack([v.astype(F32) for v in vec_rows]
                     + [jnp.zeros((D_MODEL,), F32)] * (VEC_ROWS - len(vec_rows)))
    head_of_col = jnp.arange(MEAN_WIDTH) // HEAD_DIM
    mavg = jnp.where(head_of_col[:, None] == head_of_col[None, :], 1.0 / HEAD_DIM, 0.0).astype(BF16)
    tri = causal.astype(BF16)

    def const(shape):
        zeros = (0,) * len(shape)
        return pl.BlockSpec(shape, lambda i: zeros, pipeline_mode=pl.Buffered(1))

    last = n_blocks - 1
    last_half = 2 * n_blocks - 1
    in_specs = [
        pl.BlockSpec((BLOCK, D_MODEL), lambda i: (jnp.minimum(i, last), 0)),
        pl.BlockSpec((BLOCK, D_MODEL), lambda i: (jnp.maximum(i - 1, 0), 0)),
        pl.BlockSpec((BLOCK // 2, D_MODEL),
                     lambda i: (jnp.minimum(2 * (i + 1), last_half), 0)),
        const((VEC_ROWS, D_MODEL)),
        const((D_MODEL, W_IN_COLS)),
        const((HEADS // 2, CHUNK, 2 * CHUNK)),
        const((CHUNK, GM_WIDTH)),
        const((MEAN_WIDTH, MEAN_WIDTH)),
        const((CHUNK, CHUNK)),
        const((MIX_WIDTH, D_MODEL)),
        const((D_MODEL, D_FF)),
        const((D_FF, D_MODEL)),
    ]
    return pl.pallas_call(
        functools.partial(_layer_kernel, blocks_per_seq=seq // BLOCK, n_blocks=n_blocks),
        grid=(n_blocks + 1,),
        in_specs=in_specs,
        out_specs=pl.BlockSpec((BLOCK, D_MODEL), lambda i: (jnp.maximum(i - 1, 0), 0)),
        out_shape=jax.ShapeDtypeStruct((t, D_MODEL), F32),
        scratch_shapes=[
            pltpu.VMEM((BLOCK, MAIN_COLS), F32),
            pltpu.VMEM((BLOCK, LANES), F32),
            pltpu.VMEM((BLOCK, MIX_WIDTH), BF16),
            pltpu.VMEM((BLOCK + PAD_ROWS, CONV_CH), F32),
            pltpu.VMEM((GROUPS, STATE, GROUP_WIDTH), F32),
            pltpu.VMEM((BLOCK, CONV_CH), F32),
        ],
        compiler_params=pltpu.CompilerParams(
            dimension_semantics=("arbitrary",), vmem_limit_bytes=VMEM_LIMIT_BYTES),
        name="hybrid_layer",
    )(x2d, x2d, x2d, vecs, w_all, wcat, bs_e, mavg, tri, w_out.astype(BF16), w_up.astype(BF16),
      w_down.astype(BF16))


def kernel(x, norm_mix_pre, w_in, gm_ln_w, gm_ln_b, gm_w_s, gm_b_s, conv_w, conv_b, dt_bias, a_log, d_skip, ssm_norm_w, w_out, norm_mix_post, norm_ffn_pre, w_up, w_down, norm_ffn_post):
    bsz, seq, d = x.shape
    depth = w_in.shape[0]
    x2d = x.reshape(bsz * seq, d)
    for i in range(depth):
        x2d = _layer(x2d, bsz, seq, norm_mix_pre[i], w_in[i], gm_ln_w[i], gm_ln_b[i], gm_w_s[i],
                     gm_b_s[i], conv_w[i], conv_b[i], dt_bias[i], a_log[i], d_skip[i],
                     ssm_norm_w[i], w_out[i], norm_mix_post[i], norm_ffn_pre[i], w_up[i],
                     w_down[i], norm_ffn_post[i])
    return x2d.reshape(bsz, seq, d)
```

```python
import functools
import math

import jax
import jax.numpy as jnp
from jax import lax
from jax.experimental import pallas as pl
from jax.experimental.pallas import tpu as pltpu

F32 = jnp.float32
BF16 = jnp.bfloat16

D_MODEL = 1024
GM_WIDTH = 512
HEAD_DIM = 64
HEADS = 8
CHUNK = 128
SSM_WIDTH = 512
GROUPS = 2
HEADS_PER_GROUP = HEADS // GROUPS
GROUP_WIDTH = HEADS_PER_GROUP * HEAD_DIM
STATE = 128
CONV = 4
CONV_CH = SSM_WIDTH + 2 * GROUPS * STATE
D_FF = 4 * D_MODEL
EPS = 1e-6
MAIN_COLS = 2 * GM_WIDTH + SSM_WIDTH + CONV_CH
Z_COL = 2 * GM_WIDTH
XBC_COL = Z_COL + SSM_WIDTH
MIX_WIDTH = GM_WIDTH + SSM_WIDTH
LANES = 128
PAD_ROWS = 8
LOG2E = 1.4426950408889634

VMEM_LIMIT_BYTES = 60 * 1024 * 1024

BLOCK = 512
PROJ_BLOCK = 512
FF_PIECE = 512


def _rms(x, w):
    return x * lax.rsqrt(jnp.mean(x * x, axis=-1, keepdims=True) + EPS) * w


def _split_dot_left(m, a, passes):
    pieces = []
    rem = a
    for _ in range(passes):
        piece = rem.astype(BF16)
        pieces.append(piece)
        rem = rem - piece.astype(F32)
    d = jnp.dot(m, jnp.concatenate(pieces, axis=1), preferred_element_type=F32)
    width = a.shape[1]
    acc = d[:, 0:width]
    for k in range(1, passes):
        acc = acc + d[:, k * width:(k + 1) * width]
    return acc


_GELU_A = -2.0 * math.sqrt(2.0 / math.pi) * LOG2E
_GELU_B = _GELU_A * 0.044715


def _gelu(x):
    return x / (1.0 + jnp.exp2(x * (_GELU_A + _GELU_B * (x * x))))


def _silu(x):
    return x / (1.0 + jnp.exp2(x * (-LOG2E)))


def _softplus(x):
    return jnp.maximum(x, 0.0) + jnp.log1p(jnp.exp(-jnp.abs(x)))


def _layer_kernel(
        xa_ref, xb_ref, xn_ref, nmix_ref, win_ref, wdt_ref, lnw_ref, lnb_ref, wcat_ref, bs_ref,
        convw_ref, convb_ref, dtb_ref, a_ref, dskip_ref, nrmw_ref, tri_ref,
        wout_ref, npost_ref, npre_ref, wup_ref, wdown_ref, nffn_ref,
        out_ref,
        proj_ref, dtraw_ref, mix_ref, xpad_ref, state_ref, xc_ref,
        *, blocks_per_seq, n_blocks):
    i = pl.program_id(0)
    n_chunks = BLOCK // CHUNK
    first_of_seq = lax.rem(jnp.minimum(i, n_blocks - 1), blocks_per_seq) == 0

    def in_proj_pieces(xv, dst):
        hn = _rms(xv, nmix_ref[...]).astype(BF16)

        def main(n0):
            proj_ref[dst, n0:n0 + PROJ_BLOCK] = jnp.dot(
                hn, win_ref[:, n0:n0 + PROJ_BLOCK], preferred_element_type=F32)

        def dt():
            dtraw_ref[dst, :] = jnp.dot(hn, wdt_ref[...], preferred_element_type=F32)

        return [functools.partial(main, n0) for n0 in range(0, MAIN_COLS, PROJ_BLOCK)] + [dt]

    @pl.when(i == 0)
    def _():
        mix_ref[...] = jnp.zeros(mix_ref.shape, BF16)
        for piece in in_proj_pieces(xa_ref[0:BLOCK // 2, :], slice(0, BLOCK // 2)):
            piece()

    @pl.when(first_of_seq)
    def _():
        xpad_ref[0:PAD_ROWS, :] = jnp.zeros((PAD_ROWS, CONV_CH), F32)
        state_ref[...] = jnp.zeros(state_ref.shape, F32)

    lane = lax.broadcasted_iota(jnp.int32, (CHUNK, LANES), 1)
    row = lax.broadcasted_iota(jnp.int32, (CHUNK, LANES), 0)
    causal = row >= lane
    low_half = lane < HEAD_DIM

    def pair_rhs(tile):
        return jnp.concatenate([jnp.where(low_half, tile, 0.0).astype(BF16),
                                jnp.where(low_half, 0.0, tile).astype(BF16)], axis=0)

    def gmlp(r):
        rows = slice(r * CHUNK, (r + 1) * CHUNK)

        def head_mean(a):
            tiles = []
            for c0 in range(0, GM_WIDTH, LANES):
                t = a[:, c0:c0 + LANES]
                s_all = jnp.sum(t, axis=-1, keepdims=True)
                s_low = jnp.sum(jnp.where(low_half, t, 0.0), axis=-1, keepdims=True)
                tiles.append(jnp.where(low_half, s_low, s_all - s_low) * (1.0 / HEAD_DIM))
            return jnp.concatenate(tiles, axis=1)

        v = _gelu(proj_ref[rows, GM_WIDTH:2 * GM_WIDTH])
        mu = head_mean(v)
        yield
        d = v - mu
        var = head_mean(d * d)
        yield
        vn = d * lax.rsqrt(var + EPS) * lnw_ref[...] + lnb_ref[...]
        for k in range(GM_WIDTH // LANES):
            cols = slice(k * LANES, (k + 1) * LANES)
            mixed = jnp.dot(wcat_ref[k], pair_rhs(vn[:, cols]),
                            preferred_element_type=F32) + bs_ref[:, cols]
            mix_ref[rows, cols] = (_gelu(proj_ref[rows, cols]) * mixed).astype(BF16)
        yield

    def ssd(r):
        r0 = r * CHUNK
        rows = slice(r0, r0 + CHUNK)
        prows = slice(r0 + PAD_ROWS, r0 + PAD_ROWS + CHUNK)

        dt = _softplus(dtraw_ref[rows, :] + dtb_ref[...])
        dta = dt * (a_ref[...] * LOG2E)
        a_cs = _split_dot_left(tri_ref[...], dta, 3)
        yield

        xpad_ref[prows, :] = proj_ref[rows, XBC_COL:MAIN_COLS]
        for ct in range(CONV_CH // LANES):
            cols = slice(ct * LANES, (ct + 1) * LANES)
            acc = convb_ref[:, cols] + convw_ref[CONV - 1:CONV, cols] * xpad_ref[prows, cols]
            for k in range(1, CONV):
                acc = acc + convw_ref[CONV - 1 - k:CONV - k, cols] * xpad_ref[
                    r0 + PAD_ROWS - k:r0 + PAD_ROWS - k + CHUNK, cols]
            xc_ref[rows, cols] = _silu(acc)

        total = a_cs[CHUNK - 1:CHUNK, :]
        w_end = dt * jnp.exp2(total - a_cs)
        tr = jnp.where(lane < HEADS, a_cs - jnp.log(dt) * LOG2E, pltpu.roll(w_end, HEADS, 1)).T

        b_t, cb, y_off, state = [], [], [], []
        for g in range(GROUPS):
            b_g = xc_ref[rows, SSM_WIDTH + g * STATE:SSM_WIDTH + (g + 1) * STATE]
            c_g = xc_ref[rows, SSM_WIDTH + (GROUPS + g) * STATE:
                         SSM_WIDTH + (GROUPS + g + 1) * STATE].astype(BF16)
            cb.append(lax.dot_general(c_g, b_g.astype(BF16), (((1,), (1,)), ((), ())),
                                      preferred_element_type=F32))
            state.append(state_ref[g])
            y_off.append(jnp.dot(c_g, state[g].astype(BF16), preferred_element_type=F32))
            b_t.append(b_g.T)
        yield

        y_parts = []
        for g in range(GROUPS):
            y_tiles = []
            for p in range(HEADS_PER_GROUP // 2):
                pcols = slice(p * LANES, (p + 1) * LANES)
                scores, b_scaled, e_acs = [], [], []
                for q in range(2):
                    hd = g * HEADS_PER_GROUP + 2 * p + q
                    acs_b = jnp.broadcast_to(a_cs[:, hd:hd + 1], (CHUNK, LANES))
                    seg = acs_b - tr[hd:hd + 1, :]
                    scores.append(
                        (cb[g] * jnp.exp2(jnp.where(causal, seg, -jnp.inf))).astype(BF16))
                    b_scaled.append((b_t[g] * tr[HEADS + hd:HEADS + hd + 1, :]).astype(BF16))
                    e_acs.append(jnp.exp2(acs_b))
                tcol = g * (HEADS_PER_GROUP // 2) + p
                rhs = pair_rhs(xc_ref[rows, tcol * LANES:(tcol + 1) * LANES])
                lhs = jnp.concatenate([jnp.concatenate(scores, axis=1),
                                       jnp.concatenate(b_scaled, axis=1)], axis=0)
                res = jnp.dot(lhs, rhs, preferred_element_type=F32)
                decay = jnp.where(low_half, e_acs[0], e_acs[1])
                y_tiles.append(res[0:CHUNK, :] + y_off[g][:, pcols] * decay)
                state_ref[g, :, pcols] = (state[g][:, pcols] * decay[CHUNK - 1:CHUNK, :]
                                          + res[CHUNK:2 * CHUNK, :])
            y_parts.append(jnp.concatenate(y_tiles, axis=1))

        for g in range(GROUPS):
            gcols = slice(g * GROUP_WIDTH, (g + 1) * GROUP_WIDTH)
            zs = proj_ref[rows, Z_COL + g * GROUP_WIDTH:Z_COL + (g + 1) * GROUP_WIDTH]
            yg = (y_parts[g] + dskip_ref[:, gcols] * xc_ref[rows, gcols]) * _silu(zs)
            yn = yg * lax.rsqrt(jnp.mean(yg * yg, axis=-1, keepdims=True) + EPS)
            mix_ref[rows, GM_WIDTH + g * GROUP_WIDTH:GM_WIDTH + (g + 1) * GROUP_WIDTH] = (
                yn * nrmw_ref[:, gcols]).astype(BF16)
        yield

    def mixers(r, fill):
        chains = [gmlp(r), ssd(r)]
        while chains:
            for chain in list(chains):
                try:
                    next(chain)
                except StopIteration:
                    chains.remove(chain)
                    continue
                fill()

    half_rows = BLOCK // 2
    cons = [dict() for _ in range(2)]

    def out_proj_piece(hf):
        hrows = slice(hf * half_rows, (hf + 1) * half_rows)
        o = jnp.dot(mix_ref[hrows, :], wout_ref[...], preferred_element_type=F32)
        x1 = xb_ref[hrows, :] + _rms(o, npost_ref[...])
        cons[hf]["x1"] = x1
        cons[hf]["h2"] = _rms(x1, npre_ref[...]).astype(BF16)

    ff_pieces = list(range(0, D_FF, FF_PIECE))
    full = {}

    def up_piece(f0):
        h2 = jnp.concatenate([cons[0]["h2"], cons[1]["h2"]], axis=0)
        up = jnp.dot(h2, wup_ref[:, f0:f0 + FF_PIECE], preferred_element_type=F32)
        full[f0] = jnp.square(jnp.maximum(up, 0.0)).astype(BF16)

    def down_piece(f0):
        dn = jnp.dot(full[f0], wdown_ref[f0:f0 + FF_PIECE, :], preferred_element_type=F32)
        full["acc"] = dn if "acc" not in full else full["acc"] + dn
        if f0 == ff_pieces[-1]:
            for hf in range(2):
                hrows = slice(hf * half_rows, (hf + 1) * half_rows)
                out_ref[hrows, :] = cons[hf]["x1"] + _rms(full["acc"][hrows, :], nffn_ref[...])

    out_proj_piece(0)
    queue = [functools.partial(out_proj_piece, 1)]
    mid = len(ff_pieces) // 2
    for part in (ff_pieces[:mid], ff_pieces[mid:]):
        queue += [functools.partial(up_piece, f0) for f0 in part]
        queue += [functools.partial(down_piece, f0) for f0 in part]
    per_slot = [5, 4, 4, 4]
    assert sum(per_slot) == len(queue)

    upper = slice(half_rows, BLOCK)
    proj_queue = (in_proj_pieces(xa_ref[upper, :], upper)
                  + in_proj_pieces(xn_ref[...], slice(0, half_rows)))
    proj_per_slot = len(proj_queue) // n_chunks

    for r in range(n_chunks):
        mine = [queue.pop(0) for _ in range(per_slot[r])]
        nxt = [proj_queue.pop(0) for _ in range(proj_per_slot)]
        fillers = []
        while mine or nxt:
            if mine:
                fillers.append(mine.pop(0))
            if nxt:
                fillers.append(nxt.pop(0))

        def fill(fillers=fillers):
            if fillers:
                fillers.pop(0)()

        mixers(r, fill)
        while fillers:
            fill()

    xpad_ref[0:PAD_ROWS, :] = xpad_ref[BLOCK:BLOCK + PAD_ROWS, :]


def _layer(x2d, bsz, seq, norm_mix_pre, w_in, gm_ln_w, gm_ln_b, gm_w_s, gm_b_s, conv_w, conv_b,
           dt_bias, a_log, d_skip, ssm_norm_w, w_out, norm_mix_post, norm_ffn_pre, w_up, w_down,
           norm_ffn_post):
    t = bsz * seq
    assert seq % BLOCK == 0, (seq, BLOCK)
    n_blocks = t // BLOCK
    row = lambda p: p.reshape(1, -1).astype(F32)

    w_main = w_in[:, :MAIN_COLS].astype(BF16)
    w_dt = jnp.pad(w_in[:, MAIN_COLS:], ((0, 0), (0, LANES - HEADS))).astype(BF16)
    causal = jnp.tril(jnp.ones((CHUNK, CHUNK), dtype=bool))
    w_s = jnp.where(causal[None], gm_w_s, 0.0).astype(BF16)
    wcat = w_s.reshape(HEADS // 2, 2, CHUNK, CHUNK).transpose(0, 2, 1, 3).reshape(
        HEADS // 2, CHUNK, 2 * CHUNK)
    bs_e = jnp.repeat(gm_b_s.T.astype(F32), HEAD_DIM, axis=1)
    dtb = jnp.pad(dt_bias.astype(F32), (0, LANES - HEADS)).reshape(1, LANES)
    a_pad = jnp.pad(-jnp.exp(a_log.astype(F32)), (0, LANES - HEADS)).reshape(1, LANES)
    dskip_e = jnp.repeat(d_skip.astype(F32), HEAD_DIM).reshape(1, SSM_WIDTH)
    tri = causal.astype(BF16)

    def const(shape):
        zeros = (0,) * len(shape)
        return pl.BlockSpec(shape, lambda i: zeros, pipeline_mode=pl.Buffered(1))

    last = n_blocks - 1
    last_half = 2 * n_blocks - 1
    in_specs = [
        pl.BlockSpec((BLOCK, D_MODEL), lambda i: (jnp.minimum(i, last), 0)),
        pl.BlockSpec((BLOCK, D_MODEL), lambda i: (jnp.maximum(i - 1, 0), 0)),
        pl.BlockSpec((BLOCK // 2, D_MODEL),
                     lambda i: (jnp.minimum(2 * (i + 1), last_half), 0)),
        const((1, D_MODEL)),
        const((D_MODEL, MAIN_COLS)),
        const((D_MODEL, LANES)),
        const((1, GM_WIDTH)),
        const((1, GM_WIDTH)),
        const((HEADS // 2, CHUNK, 2 * CHUNK)),
        const((CHUNK, GM_WIDTH)),
        const((CONV, CONV_CH)),
        const((1, CONV_CH)),
        const((1, LANES)),
        const((1, LANES)),
        const((1, SSM_WIDTH)),
        const((1, SSM_WIDTH)),
        const((CHUNK, CHUNK)),
        const((MIX_WIDTH, D_MODEL)),
        const((1, D_MODEL)),
        const((1, D_MODEL)),
        const((D_MODEL, D_FF)),
        const((D_FF, D_MODEL)),
        const((1, D_MODEL)),
    ]
    return pl.pallas_call(
        functools.partial(_layer_kernel, blocks_per_seq=seq // BLOCK, n_blocks=n_blocks),
        grid=(n_blocks + 1,),
        in_specs=in_specs,
        out_specs=pl.BlockSpec((BLOCK, D_MODEL), lambda i: (jnp.maximum(i - 1, 0), 0)),
        out_shape=jax.ShapeDtypeStruct((t, D_MODEL), F32),
        scratch_shapes=[
            pltpu.VMEM((BLOCK, MAIN_COLS), F32),
            pltpu.VMEM((BLOCK, LANES), F32),
            pltpu.VMEM((BLOCK, MIX_WIDTH), BF16),
            pltpu.VMEM((BLOCK + PAD_ROWS, CONV_CH), F32),
            pltpu.VMEM((GROUPS, STATE, GROUP_WIDTH), F32),
            pltpu.VMEM((BLOCK, CONV_CH), F32),
        ],
        compiler_params=pltpu.CompilerParams(
            dimension_semantics=("arbitrary",), vmem_limit_bytes=VMEM_LIMIT_BYTES),
        name="hybrid_layer",
    )(x2d, x2d, x2d, row(norm_mix_pre), w_main, w_dt, row(gm_ln_w), row(gm_ln_b), wcat, bs_e,
      conv_w.astype(F32), row(conv_b), dtb, a_pad, dskip_e, row(ssm_norm_w), tri,
      w_out.astype(BF16), row(norm_mix_post), row(norm_ffn_pre), w_up.astype(BF16),
      w_down.astype(BF16), row(norm_ffn_post))


def kernel(x, norm_mix_pre, w_in, gm_ln_w, gm_ln_b, gm_w_s, gm_b_s, conv_w, conv_b, dt_bias, a_log, d_skip, ssm_norm_w, w_out, norm_mix_post, norm_ffn_pre, w_up, w_down, norm_ffn_post):
    bsz, seq, d = x.shape
    depth = w_in.shape[0]
    x2d = x.reshape(bsz * seq, d)
    for i in range(depth):
        x2d = _layer(x2d, bsz, seq, norm_mix_pre[i], w_in[i], gm_ln_w[i], gm_ln_b[i], gm_w_s[i],
                     gm_b_s[i], conv_w[i], conv_b[i], dt_bias[i], a_log[i], d_skip[i],
                     ssm_norm_w[i], w_out[i], norm_mix_post[i], norm_ffn_pre[i], w_up[i],
                     w_down[i], norm_ffn_post[i])
    return x2d.reshape(bsz, seq, d)
```

```python
import functools
import math

import jax
import jax.numpy as jnp
from jax import lax
from jax.experimental import pallas as pl
from jax.experimental.pallas import tpu as pltpu

F32 = jnp.float32
BF16 = jnp.bfloat16

D_MODEL = 1024
GM_WIDTH = 512
HEAD_DIM = 64
HEADS = 8
CHUNK = 128
SSM_WIDTH = 512
GROUPS = 2
HEADS_PER_GROUP = HEADS // GROUPS
GROUP_WIDTH = HEADS_PER_GROUP * HEAD_DIM
STATE = 128
CONV = 4
CONV_CH = SSM_WIDTH + 2 * GROUPS * STATE
D_FF = 4 * D_MODEL
EPS = 1e-6
MAIN_COLS = 2 * GM_WIDTH + SSM_WIDTH + CONV_CH
Z_COL = 2 * GM_WIDTH
XBC_COL = Z_COL + SSM_WIDTH
MIX_WIDTH = GM_WIDTH + SSM_WIDTH
LANES = 128
MEAN_WIDTH = 256
PAD_ROWS = 8
LOG2E = 1.4426950408889634

VMEM_LIMIT_BYTES = 60 * 1024 * 1024

BLOCK = 512
PROJ_BLOCK = 1280
FF_PIECE = 512


def _rms(x, w):
    return x * lax.rsqrt(jnp.mean(x * x, axis=-1, keepdims=True) + EPS) * w


def _split_dot_left(m, a, passes):
    pieces = []
    rem = a
    for _ in range(passes):
        piece = rem.astype(BF16)
        pieces.append(piece)
        rem = rem - piece.astype(F32)
    d = jnp.dot(m, jnp.concatenate(pieces, axis=1), preferred_element_type=F32)
    width = a.shape[1]
    acc = d[:, 0:width]
    for k in range(1, passes):
        acc = acc + d[:, k * width:(k + 1) * width]
    return acc


_GELU_A = -2.0 * math.sqrt(2.0 / math.pi) * LOG2E
_GELU_B = _GELU_A * 0.044715


def _gelu(x):
    return x / (1.0 + jnp.exp2(x * (_GELU_A + _GELU_B * (x * x))))


def _silu(x):
    return x / (1.0 + jnp.exp2(x * (-LOG2E)))


def _softplus(x):
    return jnp.maximum(x, 0.0) + jnp.log1p(jnp.exp(-jnp.abs(x)))


def _layer_kernel(
        xa_ref, xb_ref, xn_ref, nmix_ref, win_ref, wdt_ref, lnw_ref, lnb_ref, wcat_ref, bs_ref,
        convw_ref, convb_ref, dtb_ref, a_ref, dskip_ref, nrmw_ref, mavg_ref, tri_ref,
        wout_ref, npost_ref, npre_ref, wup_ref, wdown_ref, nffn_ref,
        out_ref,
        proj_ref, dtraw_ref, mix_ref, xpad_ref, state_ref, xc_ref,
        *, blocks_per_seq, n_blocks):
    i = pl.program_id(0)
    n_chunks = BLOCK // CHUNK
    first_of_seq = lax.rem(jnp.minimum(i, n_blocks - 1), blocks_per_seq) == 0

    def in_proj_pieces(xv, dst):
        hn = _rms(xv, nmix_ref[...]).astype(BF16)

        def main(n0):
            proj_ref[dst, n0:n0 + PROJ_BLOCK] = jnp.dot(
                hn, win_ref[:, n0:n0 + PROJ_BLOCK], preferred_element_type=F32)

        def dt():
            dtraw_ref[dst, :] = jnp.dot(hn, wdt_ref[...], preferred_element_type=F32)

        return [functools.partial(main, n0) for n0 in range(0, MAIN_COLS, PROJ_BLOCK)] + [dt]

    @pl.when(i == 0)
    def _():
        mix_ref[...] = jnp.zeros(mix_ref.shape, BF16)
        for piece in in_proj_pieces(xa_ref[0:BLOCK // 2, :], slice(0, BLOCK // 2)):
            piece()

    @pl.when(first_of_seq)
    def _():
        xpad_ref[0:PAD_ROWS, :] = jnp.zeros((PAD_ROWS, CONV_CH), F32)
        state_ref[...] = jnp.zeros(state_ref.shape, F32)

    lane = lax.broadcasted_iota(jnp.int32, (CHUNK, LANES), 1)
    row = lax.broadcasted_iota(jnp.int32, (CHUNK, LANES), 0)
    causal = row >= lane
    low_half = lane < HEAD_DIM

    def pair_rhs(tile):
        return jnp.concatenate([jnp.where(low_half, tile, 0.0).astype(BF16),
                                jnp.where(low_half, 0.0, tile).astype(BF16)], axis=0)

    def gmlp(r):
        rows = slice(r * CHUNK, (r + 1) * CHUNK)

        def head_mean(a):
            ab = a.astype(BF16)
            col_blocks = range(0, GM_WIDTH, MEAN_WIDTH)
            stacked = jnp.concatenate([ab[:, c0:c0 + MEAN_WIDTH] for c0 in col_blocks], axis=0)
            m = jnp.dot(stacked, mavg_ref[...], preferred_element_type=F32)
            return jnp.concatenate(
                [m[k * CHUNK:(k + 1) * CHUNK, :] for k in range(len(col_blocks))], axis=1)

        v = _gelu(proj_ref[rows, GM_WIDTH:2 * GM_WIDTH])
        mu = head_mean(v)
        yield
        d = v - mu
        var = head_mean(d * d)
        yield
        vn = d * lax.rsqrt(var + EPS) * lnw_ref[...] + lnb_ref[...]
        for k in range(GM_WIDTH // LANES):
            cols = slice(k * LANES, (k + 1) * LANES)
            mixed = jnp.dot(wcat_ref[k], pair_rhs(vn[:, cols]),
                            preferred_element_type=F32) + bs_ref[:, cols]
            mix_ref[rows, cols] = (_gelu(proj_ref[rows, cols]) * mixed).astype(BF16)
        yield

    def ssd(r):
        r0 = r * CHUNK
        rows = slice(r0, r0 + CHUNK)
        prows = slice(r0 + PAD_ROWS, r0 + PAD_ROWS + CHUNK)

        dt = _softplus(dtraw_ref[rows, :] + dtb_ref[...])
        dta = dt * (a_ref[...] * LOG2E)
        a_cs = _split_dot_left(tri_ref[...], dta, 3)
        yield

        xpad_ref[prows, :] = proj_ref[rows, XBC_COL:MAIN_COLS]
        for ct in range(CONV_CH // LANES):
            cols = slice(ct * LANES, (ct + 1) * LANES)
            acc = convb_ref[:, cols] + convw_ref[CONV - 1:CONV, cols] * xpad_ref[prows, cols]
            for k in range(1, CONV):
                acc = acc + convw_ref[CONV - 1 - k:CONV - k, cols] * xpad_ref[
                    r0 + PAD_ROWS - k:r0 + PAD_ROWS - k + CHUNK, cols]
            xc_ref[rows, cols] = _silu(acc)

        total = a_cs[CHUNK - 1:CHUNK, :]
        w_end = dt * jnp.exp2(total - a_cs)
        tr = jnp.where(lane < HEADS, a_cs - jnp.log(dt) * LOG2E, pltpu.roll(w_end, HEADS, 1)).T

        b_t, cb, y_off, state = [], [], [], []
        for g in range(GROUPS):
            b_g = xc_ref[rows, SSM_WIDTH + g * STATE:SSM_WIDTH + (g + 1) * STATE]
            c_g = xc_ref[rows, SSM_WIDTH + (GROUPS + g) * STATE:
                         SSM_WIDTH + (GROUPS + g + 1) * STATE].astype(BF16)
            cb.append(lax.dot_general(c_g, b_g.astype(BF16), (((1,), (1,)), ((), ())),
                                      preferred_element_type=F32))
            state.append(state_ref[g])
            y_off.append(jnp.dot(c_g, state[g].astype(BF16), preferred_element_type=F32))
            b_t.append(b_g.T)
        yield

        y_parts = []
        for g in range(GROUPS):
            y_tiles = []
            for p in range(HEADS_PER_GROUP // 2):
                pcols = slice(p * LANES, (p + 1) * LANES)
                scores, b_scaled, e_acs = [], [], []
                for q in range(2):
                    hd = g * HEADS_PER_GROUP + 2 * p + q
                    acs_b = jnp.broadcast_to(a_cs[:, hd:hd + 1], (CHUNK, LANES))
                    seg = acs_b - tr[hd:hd + 1, :]
                    scores.append(
                        (cb[g] * jnp.exp2(jnp.where(causal, seg, -jnp.inf))).astype(BF16))
                    b_scaled.append((b_t[g] * tr[HEADS + hd:HEADS + hd + 1, :]).astype(BF16))
                    e_acs.append(jnp.exp2(acs_b))
                tcol = g * (HEADS_PER_GROUP // 2) + p
                rhs = pair_rhs(xc_ref[rows, tcol * LANES:(tcol + 1) * LANES])
                lhs = jnp.concatenate([jnp.concatenate(scores, axis=1),
                                       jnp.concatenate(b_scaled, axis=1)], axis=0)
                res = jnp.dot(lhs, rhs, preferred_element_type=F32)
                decay = jnp.where(low_half, e_acs[0], e_acs[1])
                y_tiles.append(res[0:CHUNK, :] + y_off[g][:, pcols] * decay)
                state_ref[g, :, pcols] = (state[g][:, pcols] * decay[CHUNK - 1:CHUNK, :]
                                          + res[CHUNK:2 * CHUNK, :])
            y_parts.append(jnp.concatenate(y_tiles, axis=1))

        for g in range(GROUPS):
            gcols = slice(g * GROUP_WIDTH, (g + 1) * GROUP_WIDTH)
            zs = proj_ref[rows, Z_COL + g * GROUP_WIDTH:Z_COL + (g + 1) * GROUP_WIDTH]
            yg = (y_parts[g] + dskip_ref[:, gcols] * xc_ref[rows, gcols]) * _silu(zs)
            yn = yg * lax.rsqrt(jnp.mean(yg * yg, axis=-1, keepdims=True) + EPS)
            mix_ref[rows, GM_WIDTH + g * GROUP_WIDTH:GM_WIDTH + (g + 1) * GROUP_WIDTH] = (
                yn * nrmw_ref[:, gcols]).astype(BF16)
        yield

    def mixers(r, fill):
        chains = [gmlp(r), ssd(r)]
        while chains:
            for chain in list(chains):
                try:
                    next(chain)
                except StopIteration:
                    chains.remove(chain)
                    continue
                fill()

    half_rows = BLOCK // 2
    cons = [dict() for _ in range(2)]

    def out_proj_piece(hf):
        hrows = slice(hf * half_rows, (hf + 1) * half_rows)
        o = jnp.dot(mix_ref[hrows, :], wout_ref[...], preferred_element_type=F32)
        x1 = xb_ref[hrows, :] + _rms(o, npost_ref[...])
        cons[hf]["x1"] = x1
        cons[hf]["h2"] = _rms(x1, npre_ref[...]).astype(BF16)

    ff_pieces = list(range(0, D_FF, FF_PIECE))
    full = {}

    def up_piece(f0):
        h2 = jnp.concatenate([cons[0]["h2"], cons[1]["h2"]], axis=0)
        up = jnp.dot(h2, wup_ref[:, f0:f0 + FF_PIECE], preferred_element_type=F32)
        full[f0] = jnp.square(jnp.maximum(up, 0.0)).astype(BF16)

    def down_piece(f0):
        dn = jnp.dot(full[f0], wdown_ref[f0:f0 + FF_PIECE, :], preferred_element_type=F32)
        full["acc"] = dn if "acc" not in full else full["acc"] + dn
        if f0 == ff_pieces[-1]:
            for hf in range(2):
                hrows = slice(hf * half_rows, (hf + 1) * half_rows)
                out_ref[hrows, :] = cons[hf]["x1"] + _rms(full["acc"][hrows, :], nffn_ref[...])

    out_proj_piece(0)
    queue = [functools.partial(out_proj_piece, 1)]
    mid = len(ff_pieces) // 2
    for part in (ff_pieces[:mid], ff_pieces[mid:]):
        queue += [functools.partial(up_piece, f0) for f0 in part]
        queue += [functools.partial(down_piece, f0) for f0 in part]
    per_slot = [5, 4, 4, 4]
    assert sum(per_slot) == len(queue)

    upper = slice(half_rows, BLOCK)
    proj_queue = (in_proj_pieces(xa_ref[upper, :], upper)
                  + in_proj_pieces(xn_ref[...], slice(0, half_rows)))
    half_pieces = len(proj_queue) // 2
    proj_per_slot = 2 * [half_pieces - half_pieces // 2, half_pieces // 2]

    for r in range(n_chunks):
        mine = [queue.pop(0) for _ in range(per_slot[r])]
        nxt = [proj_queue.pop(0) for _ in range(proj_per_slot[r])]
        fillers = []
        while mine or nxt:
            if mine:
                fillers.append(mine.pop(0))
            if nxt:
                fillers.append(nxt.pop(0))

        def fill(fillers=fillers):
            if fillers:
                fillers.pop(0)()

        mixers(r, fill)
        while fillers:
            fill()

    xpad_ref[0:PAD_ROWS, :] = xpad_ref[BLOCK:BLOCK + PAD_ROWS, :]


def _layer(x2d, bsz, seq, norm_mix_pre, w_in, gm_ln_w, gm_ln_b, gm_w_s, gm_b_s, conv_w, conv_b,
           dt_bias, a_log, d_skip, ssm_norm_w, w_out, norm_mix_post, norm_ffn_pre, w_up, w_down,
           norm_ffn_post):
    t = bsz * seq
    assert seq % BLOCK == 0, (seq, BLOCK)
    n_blocks = t // BLOCK
    row = lambda p: p.reshape(1, -1).astype(F32)

    w_main = w_in[:, :MAIN_COLS].astype(BF16)
    w_dt = jnp.pad(w_in[:, MAIN_COLS:], ((0, 0), (0, LANES - HEADS))).astype(BF16)
    causal = jnp.tril(jnp.ones((CHUNK, CHUNK), dtype=bool))
    w_s = jnp.where(causal[None], gm_w_s, 0.0).astype(BF16)
    wcat = w_s.reshape(HEADS // 2, 2, CHUNK, CHUNK).transpose(0, 2, 1, 3).reshape(
        HEADS // 2, CHUNK, 2 * CHUNK)
    bs_e = jnp.repeat(gm_b_s.T.astype(F32), HEAD_DIM, axis=1)
    dtb = jnp.pad(dt_bias.astype(F32), (0, LANES - HEADS)).reshape(1, LANES)
    a_pad = jnp.pad(-jnp.exp(a_log.astype(F32)), (0, LANES - HEADS)).reshape(1, LANES)
    dskip_e = jnp.repeat(d_skip.astype(F32), HEAD_DIM).reshape(1, SSM_WIDTH)
    head_of_col = jnp.arange(MEAN_WIDTH) // HEAD_DIM
    mavg = jnp.where(head_of_col[:, None] == head_of_col[None, :], 1.0 / HEAD_DIM, 0.0).astype(BF16)
    tri = causal.astype(BF16)

    def const(shape):
        zeros = (0,) * len(shape)
        return pl.BlockSpec(shape, lambda i: zeros, pipeline_mode=pl.Buffered(1))

    last = n_blocks - 1
    last_half = 2 * n_blocks - 1
    in_specs = [
        pl.BlockSpec((BLOCK, D_MODEL), lambda i: (jnp.minimum(i, last), 0)),
        pl.BlockSpec((BLOCK, D_MODEL), lambda i: (jnp.maximum(i - 1, 0), 0)),
        pl.BlockSpec((BLOCK // 2, D_MODEL),
                     lambda i: (jnp.minimum(2 * (i + 1), last_half), 0)),
        const((1, D_MODEL)),
        const((D_MODEL, MAIN_COLS)),
        const((D_MODEL, LANES)),
        const((1, GM_WIDTH)),
        const((1, GM_WIDTH)),
        const((HEADS // 2, CHUNK, 2 * CHUNK)),
        const((CHUNK, GM_WIDTH)),
        const((CONV, CONV_CH)),
        const((1, CONV_CH)),
        const((1, LANES)),
        const((1, LANES)),
        const((1, SSM_WIDTH)),
        const((1, SSM_WIDTH)),
        const((MEAN_WIDTH, MEAN_WIDTH)),
        const((CHUNK, CHUNK)),
        const((MIX_WIDTH, D_MODEL)),
        const((1, D_MODEL)),
        const((1, D_MODEL)),
        const((D_MODEL, D_FF)),
        const((D_FF, D_MODEL)),
        const((1, D_MODEL)),
    ]
    return pl.pallas_call(
        functools.partial(_layer_kernel, blocks_per_seq=seq // BLOCK, n_blocks=n_blocks),
        grid=(n_blocks + 1,),
        in_specs=in_specs,
        out_specs=pl.BlockSpec((BLOCK, D_MODEL), lambda i: (jnp.maximum(i - 1, 0), 0)),
        out_shape=jax.ShapeDtypeStruct((t, D_MODEL), F32),
        scratch_shapes=[
            pltpu.VMEM((BLOCK, MAIN_COLS), F32),
            pltpu.VMEM((BLOCK, LANES), F32),
            pltpu.VMEM((BLOCK, MIX_WIDTH), BF16),
            pltpu.VMEM((BLOCK + PAD_ROWS, CONV_CH), F32),
            pltpu.VMEM((GROUPS, STATE, GROUP_WIDTH), F32),
            pltpu.VMEM((BLOCK, CONV_CH), F32),
        ],
        compiler_params=pltpu.CompilerParams(
            dimension_semantics=("arbitrary",), vmem_limit_bytes=VMEM_LIMIT_BYTES),
        name="hybrid_layer",
    )(x2d, x2d, x2d, row(norm_mix_pre), w_main, w_dt, row(gm_ln_w), row(gm_ln_b), wcat, bs_e,
      conv_w.astype(F32), row(conv_b), dtb, a_pad, dskip_e, row(ssm_norm_w), mavg, tri,
      w_out.astype(BF16), row(norm_mix_post), row(norm_ffn_pre), w_up.astype(BF16),
      w_down.astype(BF16), row(norm_ffn_post))


def kernel(x, norm_mix_pre, w_in, gm_ln_w, gm_ln_b, gm_w_s, gm_b_s, conv_w, conv_b, dt_bias, a_log, d_skip, ssm_norm_w, w_out, norm_mix_post, norm_ffn_pre, w_up, w_down, norm_ffn_post):
    bsz, seq, d = x.shape
    depth = w_in.shape[0]
    x2d = x.reshape(bsz * seq, d)
    for i in range(depth):
        x2d = _layer(x2d, bsz, seq, norm_mix_pre[i], w_in[i], gm_ln_w[i], gm_ln_b[i], gm_w_s[i],
                     gm_b_s[i], conv_w[i], conv_b[i], dt_bias[i], a_log[i], d_skip[i],
                     ssm_norm_w[i], w_out[i], norm_mix_post[i], norm_ffn_pre[i], w_up[i],
                     w_down[i], norm_ffn_post[i])
    return x2d.reshape(bsz, seq, d)
```

```python
import functools
import math

import jax
import jax.numpy as jnp
from jax import lax
from jax.experimental import pallas as pl
from jax.experimental.pallas import tpu as pltpu

F32 = jnp.float32
BF16 = jnp.bfloat16

D_MODEL = 1024
GM_WIDTH = 512
HEAD_DIM = 64
HEADS = 8
CHUNK = 128
SSM_WIDTH = 512
GROUPS = 2
HEADS_PER_GROUP = HEADS // GROUPS
GROUP_WIDTH = HEADS_PER_GROUP * HEAD_DIM
STATE = 128
CONV = 4
CONV_CH = SSM_WIDTH + 2 * GROUPS * STATE
D_FF = 4 * D_MODEL
EPS = 1e-6
MAIN_COLS = 2 * GM_WIDTH + SSM_WIDTH + CONV_CH
Z_COL = 2 * GM_WIDTH
XBC_COL = Z_COL + SSM_WIDTH
MIX_WIDTH = GM_WIDTH + SSM_WIDTH
LANES = 128
MEAN_WIDTH = 256
PAD_ROWS = 8
LOG2E = 1.4426950408889634

VMEM_LIMIT_BYTES = 60 * 1024 * 1024

BLOCK = 512
PROJ_BLOCK = 512
FF_PIECE = 512


def _rms(x, w):
    return x * lax.rsqrt(jnp.mean(x * x, axis=-1, keepdims=True) + EPS) * w


def _split_dot_left(m, a, passes):
    pieces = []
    rem = a
    for _ in range(passes):
        piece = rem.astype(BF16)
        pieces.append(piece)
        rem = rem - piece.astype(F32)
    d = jnp.dot(m, jnp.concatenate(pieces, axis=1), preferred_element_type=F32)
    width = a.shape[1]
    acc = d[:, 0:width]
    for k in range(1, passes):
        acc = acc + d[:, k * width:(k + 1) * width]
    return acc


_GELU_A = -2.0 * math.sqrt(2.0 / math.pi) * LOG2E
_GELU_B = _GELU_A * 0.044715


def _gelu(x):
    return x / (1.0 + jnp.exp2(x * (_GELU_A + _GELU_B * (x * x))))


def _silu(x):
    return x / (1.0 + jnp.exp2(x * (-LOG2E)))


def _softplus(x):
    return jnp.maximum(x, 0.0) + jnp.log1p(jnp.exp(-jnp.abs(x)))


def _layer_kernel(
        xa_ref, xb_ref, xn_ref, nmix_ref, win_ref, wdt_ref, lnw_ref, lnb_ref, wcat_ref, bs_ref,
        convw_ref, convb_ref, dtb_ref, a_ref, dskip_ref, nrmw_ref, mavg_ref, tri_ref,
        wout_ref, npost_ref, npre_ref, wup_ref, wdown_ref, nffn_ref,
        out_ref,
        proj_ref, dtraw_ref, mix_ref, xpad_ref, state_ref, xc_ref,
        *, blocks_per_seq, n_blocks):
    i = pl.program_id(0)
    n_chunks = BLOCK // CHUNK
    first_of_seq = lax.rem(jnp.minimum(i, n_blocks - 1), blocks_per_seq) == 0

    def in_proj_pieces(xv, dst):
        hn = _rms(xv, nmix_ref[...]).astype(BF16)

        def main(n0):
            proj_ref[dst, n0:n0 + PROJ_BLOCK] = jnp.dot(
                hn, win_ref[:, n0:n0 + PROJ_BLOCK], preferred_element_type=F32)

        def dt():
            dtraw_ref[dst, :] = jnp.dot(hn, wdt_ref[...], preferred_element_type=F32)

        return [functools.partial(main, n0) for n0 in range(0, MAIN_COLS, PROJ_BLOCK)] + [dt]

    @pl.when(i == 0)
    def _():
        mix_ref[...] = jnp.zeros(mix_ref.shape, BF16)
        for piece in in_proj_pieces(xa_ref[0:BLOCK // 2, :], slice(0, BLOCK // 2)):
            piece()

    @pl.when(first_of_seq)
    def _():
        xpad_ref[0:PAD_ROWS, :] = jnp.zeros((PAD_ROWS, CONV_CH), F32)
        state_ref[...] = jnp.zeros(state_ref.shape, F32)

    lane = lax.broadcasted_iota(jnp.int32, (CHUNK, LANES), 1)
    row = lax.broadcasted_iota(jnp.int32, (CHUNK, LANES), 0)
    causal = row >= lane
    low_half = lane < HEAD_DIM

    def pair_rhs(tile):
        return jnp.concatenate([jnp.where(low_half, tile, 0.0).astype(BF16),
                                jnp.where(low_half, 0.0, tile).astype(BF16)], axis=0)

    def gmlp(r):
        rows = slice(r * CHUNK, (r + 1) * CHUNK)

        def head_mean(a):
            ab = a.astype(BF16)
            col_blocks = range(0, GM_WIDTH, MEAN_WIDTH)
            stacked = jnp.concatenate([ab[:, c0:c0 + MEAN_WIDTH] for c0 in col_blocks], axis=0)
            m = jnp.dot(stacked, mavg_ref[...], preferred_element_type=F32)
            return jnp.concatenate(
                [m[k * CHUNK:(k + 1) * CHUNK, :] for k in range(len(col_blocks))], axis=1)

        v = _gelu(proj_ref[rows, GM_WIDTH:2 * GM_WIDTH])
        mu = head_mean(v)
        yield
        d = v - mu
        var = head_mean(d * d)
        yield
        vn = d * lax.rsqrt(var + EPS) * lnw_ref[...] + lnb_ref[...]
        for k in range(GM_WIDTH // LANES):
            cols = slice(k * LANES, (k + 1) * LANES)
            mixed = jnp.dot(wcat_ref[k], pair_rhs(vn[:, cols]),
                            preferred_element_type=F32) + bs_ref[:, cols]
            mix_ref[rows, cols] = (_gelu(proj_ref[rows, cols]) * mixed).astype(BF16)
        yield

    def ssd(r):
        r0 = r * CHUNK
        rows = slice(r0, r0 + CHUNK)
        prows = slice(r0 + PAD_ROWS, r0 + PAD_ROWS + CHUNK)

        dt = _softplus(dtraw_ref[rows, :] + dtb_ref[...])
        dta = dt * (a_ref[...] * LOG2E)
        a_cs = _split_dot_left(tri_ref[...], dta, 3)
        yield

        xpad_ref[prows, :] = proj_ref[rows, XBC_COL:MAIN_COLS]
        for ct in range(CONV_CH // LANES):
            cols = slice(ct * LANES, (ct + 1) * LANES)
            acc = convb_ref[:, cols] + convw_ref[CONV - 1:CONV, cols] * xpad_ref[prows, cols]
            for k in range(1, CONV):
                acc = acc + convw_ref[CONV - 1 - k:CONV - k, cols] * xpad_ref[
                    r0 + PAD_ROWS - k:r0 + PAD_ROWS - k + CHUNK, cols]
            xc_ref[rows, cols] = _silu(acc)

        total = a_cs[CHUNK - 1:CHUNK, :]
        w_end = dt * jnp.exp2(total - a_cs)
        tr = jnp.where(lane < HEADS, a_cs - jnp.log(dt) * LOG2E, pltpu.roll(w_end, HEADS, 1)).T

        b_t, cb, y_off, state = [], [], [], []
        for g in range(GROUPS):
            b_g = xc_ref[rows, SSM_WIDTH + g * STATE:SSM_WIDTH + (g + 1) * STATE]
            c_g = xc_ref[rows, SSM_WIDTH + (GROUPS + g) * STATE:
                         SSM_WIDTH + (GROUPS + g + 1) * STATE].astype(BF16)
            cb.append(lax.dot_general(c_g, b_g.astype(BF16), (((1,), (1,)), ((), ())),
                                      preferred_element_type=F32))
            state.append(state_ref[g])
            y_off.append(jnp.dot(c_g, state[g].astype(BF16), preferred_element_type=F32))
            b_t.append(b_g.T)
        yield

        y_parts = []
        for g in range(GROUPS):
            y_tiles = []
            for p in range(HEADS_PER_GROUP // 2):
                pcols = slice(p * LANES, (p + 1) * LANES)
                scores, b_scaled, e_acs = [], [], []
                for q in range(2):
                    hd = g * HEADS_PER_GROUP + 2 * p + q
                    acs_b = jnp.broadcast_to(a_cs[:, hd:hd + 1], (CHUNK, LANES))
                    seg = acs_b - tr[hd:hd + 1, :]
                    scores.append(
                        (cb[g] * jnp.exp2(jnp.where(causal, seg, -jnp.inf))).astype(BF16))
                    b_scaled.append((b_t[g] * tr[HEADS + hd:HEADS + hd + 1, :]).astype(BF16))
                    e_acs.append(jnp.exp2(acs_b))
                tcol = g * (HEADS_PER_GROUP // 2) + p
                rhs = pair_rhs(xc_ref[rows, tcol * LANES:(tcol + 1) * LANES])
                lhs = jnp.concatenate([jnp.concatenate(scores, axis=1),
                                       jnp.concatenate(b_scaled, axis=1)], axis=0)
                res = jnp.dot(lhs, rhs, preferred_element_type=F32)
                decay = jnp.where(low_half, e_acs[0], e_acs[1])
                y_tiles.append(res[0:CHUNK, :] + y_off[g][:, pcols] * decay)
                state_ref[g, :, pcols] = (state[g][:, pcols] * decay[CHUNK - 1:CHUNK, :]
                                          + res[CHUNK:2 * CHUNK, :])
            y_parts.append(jnp.concatenate(y_tiles, axis=1))

        for g in range(GROUPS):
            gcols = slice(g * GROUP_WIDTH, (g + 1) * GROUP_WIDTH)
            zs = proj_ref[rows, Z_COL + g * GROUP_WIDTH:Z_COL + (g + 1) * GROUP_WIDTH]
            yg = (y_parts[g] + dskip_ref[:, gcols] * xc_ref[rows, gcols]) * _silu(zs)
            yn = yg * lax.rsqrt(jnp.mean(yg * yg, axis=-1, keepdims=True) + EPS)
            mix_ref[rows, GM_WIDTH + g * GROUP_WIDTH:GM_WIDTH + (g + 1) * GROUP_WIDTH] = (
                yn * nrmw_ref[:, gcols]).astype(BF16)
        yield

    def mixers(r, fill):
        chains = [gmlp(r), ssd(r)]
        while chains:
            for chain in list(chains):
                try:
                    next(chain)
                except StopIteration:
                    chains.remove(chain)
                    continue
                fill()

    half_rows = BLOCK // 2
    cons = [dict() for _ in range(2)]

    def out_proj_piece(hf):
        hrows = slice(hf * half_rows, (hf + 1) * half_rows)
        o = jnp.dot(mix_ref[hrows, :], wout_ref[...], preferred_element_type=F32)
        x1 = xb_ref[hrows, :] + _rms(o, npost_ref[...])
        cons[hf]["x1"] = x1
        cons[hf]["h2"] = _rms(x1, npre_ref[...]).astype(BF16)

    ff_pieces = list(range(0, D_FF, FF_PIECE))
    full = {}

    def up_piece(f0):
        h2 = jnp.concatenate([cons[0]["h2"], cons[1]["h2"]], axis=0)
        up = jnp.dot(h2, wup_ref[:, f0:f0 + FF_PIECE], preferred_element_type=F32)
        full[f0] = jnp.square(jnp.maximum(up, 0.0)).astype(BF16)

    def down_piece(f0):
        dn = jnp.dot(full[f0], wdown_ref[f0:f0 + FF_PIECE, :], preferred_element_type=F32)
        full["acc"] = dn if "acc" not in full else full["acc"] + dn
        if f0 == ff_pieces[-1]:
            for hf in range(2):
                hrows = slice(hf * half_rows, (hf + 1) * half_rows)
                out_ref[hrows, :] = cons[hf]["x1"] + _rms(full["acc"][hrows, :], nffn_ref[...])

    out_proj_piece(0)
    queue = [functools.partial(out_proj_piece, 1)]
    mid = len(ff_pieces) // 2
    for part in (ff_pieces[:mid], ff_pieces[mid:]):
        queue += [functools.partial(up_piece, f0) for f0 in part]
        queue += [functools.partial(down_piece, f0) for f0 in part]
    per_slot = [6, 4, 4, 3]
    assert sum(per_slot) == len(queue)

    upper = slice(half_rows, BLOCK)
    proj_queue = (in_proj_pieces(xa_ref[upper, :], upper)
                  + in_proj_pieces(xn_ref[...], slice(0, half_rows)))
    proj_per_slot = len(proj_queue) // n_chunks

    for r in range(n_chunks):
        mine = [queue.pop(0) for _ in range(per_slot[r])]
        nxt = [proj_queue.pop(0) for _ in range(proj_per_slot)]
        fillers = []
        if r + 1 == n_chunks:
            fillers, mine, nxt = mine + nxt, [], []
        while mine or nxt:
            if mine:
                fillers.append(mine.pop(0))
            if nxt:
                fillers.append(nxt.pop(0))

        def fill(fillers=fillers):
            if fillers:
                fillers.pop(0)()

        mixers(r, fill)
        while fillers:
            fill()

    xpad_ref[0:PAD_ROWS, :] = xpad_ref[BLOCK:BLOCK + PAD_ROWS, :]


def _layer(x2d, bsz, seq, norm_mix_pre, w_in, gm_ln_w, gm_ln_b, gm_w_s, gm_b_s, conv_w, conv_b,
           dt_bias, a_log, d_skip, ssm_norm_w, w_out, norm_mix_post, norm_ffn_pre, w_up, w_down,
           norm_ffn_post):
    t = bsz * seq
    assert seq % BLOCK == 0, (seq, BLOCK)
    n_blocks = t // BLOCK
    row = lambda p: p.reshape(1, -1).astype(F32)

    w_main = w_in[:, :MAIN_COLS].astype(BF16)
    w_dt = jnp.pad(w_in[:, MAIN_COLS:], ((0, 0), (0, LANES - HEADS))).astype(BF16)
    causal = jnp.tril(jnp.ones((CHUNK, CHUNK), dtype=bool))
    w_s = jnp.where(causal[None], gm_w_s, 0.0).astype(BF16)
    wcat = w_s.reshape(HEADS // 2, 2, CHUNK, CHUNK).transpose(0, 2, 1, 3).reshape(
        HEADS // 2, CHUNK, 2 * CHUNK)
    bs_e = jnp.repeat(gm_b_s.T.astype(F32), HEAD_DIM, axis=1)
    dtb = jnp.pad(dt_bias.astype(F32), (0, LANES - HEADS)).reshape(1, LANES)
    a_pad = jnp.pad(-jnp.exp(a_log.astype(F32)), (0, LANES - HEADS)).reshape(1, LANES)
    dskip_e = jnp.repeat(d_skip.astype(F32), HEAD_DIM).reshape(1, SSM_WIDTH)
    head_of_col = jnp.arange(MEAN_WIDTH) // HEAD_DIM
    mavg = jnp.where(head_of_col[:, None] == head_of_col[None, :], 1.0 / HEAD_DIM, 0.0).astype(BF16)
    tri = causal.astype(BF16)

    def const(shape):
        zeros = (0,) * len(shape)
        return pl.BlockSpec(shape, lambda i: zeros, pipeline_mode=pl.Buffered(1))

    last = n_blocks - 1
    last_half = 2 * n_blocks - 1
    in_specs = [
        pl.BlockSpec((BLOCK, D_MODEL), lambda i: (jnp.minimum(i, last), 0)),
        pl.BlockSpec((BLOCK, D_MODEL), lambda i: (jnp.maximum(i - 1, 0), 0)),
        pl.BlockSpec((BLOCK // 2, D_MODEL),
                     lambda i: (jnp.minimum(2 * (i + 1), last_half), 0)),
        const((1, D_MODEL)),
        const((D_MODEL, MAIN_COLS)),
        const((D_MODEL, LANES)),
        const((1, GM_WIDTH)),
        const((1, GM_WIDTH)),
        const((HEADS // 2, CHUNK, 2 * CHUNK)),
        const((CHUNK, GM_WIDTH)),
        const((CONV, CONV_CH)),
        const((1, CONV_CH)),
        const((1, LANES)),
        const((1, LANES)),
        const((1, SSM_WIDTH)),
        const((1, SSM_WIDTH)),
        const((MEAN_WIDTH, MEAN_WIDTH)),
        const((CHUNK, CHUNK)),
        const((MIX_WIDTH, D_MODEL)),
        const((1, D_MODEL)),
        const((1, D_MODEL)),
        const((D_MODEL, D_FF)),
        const((D_FF, D_MODEL)),
        const((1, D_MODEL)),
    ]
    return pl.pallas_call(
        functools.partial(_layer_kernel, blocks_per_seq=seq // BLOCK, n_blocks=n_blocks),
        grid=(n_blocks + 1,),
        in_specs=in_specs,
        out_specs=pl.BlockSpec((BLOCK, D_MODEL), lambda i: (jnp.maximum(i - 1, 0), 0)),
        out_shape=jax.ShapeDtypeStruct((t, D_MODEL), F32),
        scratch_shapes=[
            pltpu.VMEM((BLOCK, MAIN_COLS), F32),
            pltpu.VMEM((BLOCK, LANES), F32),
            pltpu.VMEM((BLOCK, MIX_WIDTH), BF16),
            pltpu.VMEM((BLOCK + PAD_ROWS, CONV_CH), F32),
            pltpu.VMEM((GROUPS, STATE, GROUP_WIDTH), F32),
            pltpu.VMEM((BLOCK, CONV_CH), F32),
        ],
        compiler_params=pltpu.CompilerParams(
            dimension_semantics=("arbitrary",), vmem_limit_bytes=VMEM_LIMIT_BYTES),
        name="hybrid_layer",
    )(x2d, x2d, x2d, row(norm_mix_pre), w_main, w_dt, row(gm_ln_w), row(gm_ln_b), wcat, bs_e,
      conv_w.astype(F32), row(conv_b), dtb, a_pad, dskip_e, row(ssm_norm_w), mavg, tri,
      w_out.astype(BF16), row(norm_mix_post), row(norm_ffn_pre), w_up.astype(BF16),
      w_down.astype(BF16), row(norm_ffn_post))


def kernel(x, norm_mix_pre, w_in, gm_ln_w, gm_ln_b, gm_w_s, gm_b_s, conv_w, conv_b, dt_bias, a_log, d_skip, ssm_norm_w, w_out, norm_mix_post, norm_ffn_pre, w_up, w_down, norm_ffn_post):
    bsz, seq, d = x.shape
    depth = w_in.shape[0]
    x2d = x.reshape(bsz * seq, d)
    for i in range(depth):
        x2d = _layer(x2d, bsz, seq, norm_mix_pre[i], w_in[i], gm_ln_w[i], gm_ln_b[i], gm_w_s[i],
                     gm_b_s[i], conv_w[i], conv_b[i], dt_bias[i], a_log[i], d_skip[i],
                     ssm_norm_w[i], w_out[i], norm_mix_post[i], norm_ffn_pre[i], w_up[i],
                     w_down[i], norm_ffn_post[i])
    return x2d.reshape(bsz, seq, d)
```

```python
import functools
import math

import jax
import jax.numpy as jnp
from jax import lax
from jax.experimental import pallas as pl
from jax.experimental.pallas import tpu as pltpu

F32 = jnp.float32
BF16 = jnp.bfloat16

D_MODEL = 1024
GM_WIDTH = 512
HEAD_DIM = 64
HEADS = 8
CHUNK = 128
SSM_WIDTH = 512
GROUPS = 2
HEADS_PER_GROUP = HEADS // GROUPS
GROUP_WIDTH = HEADS_PER_GROUP * HEAD_DIM
STATE = 128
CONV = 4
CONV_CH = SSM_WIDTH + 2 * GROUPS * STATE
D_FF = 4 * D_MODEL
EPS = 1e-6
MAIN_COLS = 2 * GM_WIDTH + SSM_WIDTH + CONV_CH
Z_COL = 2 * GM_WIDTH
XBC_COL = Z_COL + SSM_WIDTH
MIX_WIDTH = GM_WIDTH + SSM_WIDTH
LANES = 128
MEAN_WIDTH = 256
PAD_ROWS = 8
LOG2E = 1.4426950408889634

VMEM_LIMIT_BYTES = 60 * 1024 * 1024

BLOCK = 512
PROJ_BLOCK = 512
FF_PIECE = 512


def _rms(x, w):
    return x * lax.rsqrt(jnp.mean(x * x, axis=-1, keepdims=True) + EPS) * w


def _split_dot_left(m, a, passes):
    pieces = []
    rem = a
    for _ in range(passes):
        piece = rem.astype(BF16)
        pieces.append(piece)
        rem = rem - piece.astype(F32)
    d = jnp.dot(m, jnp.concatenate(pieces, axis=1), preferred_element_type=F32)
    width = a.shape[1]
    acc = d[:, 0:width]
    for k in range(1, passes):
        acc = acc + d[:, k * width:(k + 1) * width]
    return acc


_GELU_A = -2.0 * math.sqrt(2.0 / math.pi) * LOG2E
_GELU_B = _GELU_A * 0.044715


def _gelu(x):
    return x / (1.0 + jnp.exp2(x * (_GELU_A + _GELU_B * (x * x))))


def _silu(x):
    return x / (1.0 + jnp.exp2(x * (-LOG2E)))


def _softplus(x):
    return jnp.maximum(x, 0.0) + jnp.log1p(jnp.exp(-jnp.abs(x)))


def _layer_kernel(
        xa_ref, xn_ref, nmix_ref, win_ref, wdt_ref, lnw_ref, lnb_ref, wcat_ref, bs_ref,
        convw_ref, convb_ref, dtb_ref, a_ref, dskip_ref, nrmw_ref, mavg_ref, tri_ref,
        wout_ref, npost_ref, npre_ref, wup_ref, wdown_ref, nffn_ref,
        out_ref,
        proj_ref, dtraw_ref, mix_ref, xpad_ref, state_ref, xc_ref, xres_ref,
        *, blocks_per_seq, n_blocks):
    i = pl.program_id(0)
    n_chunks = BLOCK // CHUNK
    first_of_seq = lax.rem(jnp.minimum(i, n_blocks - 1), blocks_per_seq) == 0

    def in_proj_pieces(xv, dst):
        hn = _rms(xv, nmix_ref[...]).astype(BF16)

        def main(n0):
            proj_ref[dst, n0:n0 + PROJ_BLOCK] = jnp.dot(
                hn, win_ref[:, n0:n0 + PROJ_BLOCK], preferred_element_type=F32)

        def dt():
            dtraw_ref[dst, :] = jnp.dot(hn, wdt_ref[...], preferred_element_type=F32)

        return [functools.partial(main, n0) for n0 in range(0, MAIN_COLS, PROJ_BLOCK)] + [dt]

    @pl.when(i == 0)
    def _():
        mix_ref[...] = jnp.zeros(mix_ref.shape, BF16)
        xres_ref[...] = jnp.zeros(xres_ref.shape, F32)
        for piece in in_proj_pieces(xa_ref[0:BLOCK // 2, :], slice(0, BLOCK // 2)):
            piece()

    @pl.when(first_of_seq)
    def _():
        xpad_ref[0:PAD_ROWS, :] = jnp.zeros((PAD_ROWS, CONV_CH), F32)
        state_ref[...] = jnp.zeros(state_ref.shape, F32)

    lane = lax.broadcasted_iota(jnp.int32, (CHUNK, LANES), 1)
    row = lax.broadcasted_iota(jnp.int32, (CHUNK, LANES), 0)
    causal = row >= lane
    low_half = lane < HEAD_DIM

    def pair_rhs(tile):
        return jnp.concatenate([jnp.where(low_half, tile, 0.0).astype(BF16),
                                jnp.where(low_half, 0.0, tile).astype(BF16)], axis=0)

    def gmlp(r):
        rows = slice(r * CHUNK, (r + 1) * CHUNK)

        def head_mean(a):
            ab = a.astype(BF16)
            col_blocks = range(0, GM_WIDTH, MEAN_WIDTH)
            stacked = jnp.concatenate([ab[:, c0:c0 + MEAN_WIDTH] for c0 in col_blocks], axis=0)
            m = jnp.dot(stacked, mavg_ref[...], preferred_element_type=F32)
            return jnp.concatenate(
                [m[k * CHUNK:(k + 1) * CHUNK, :] for k in range(len(col_blocks))], axis=1)

        v = _gelu(proj_ref[rows, GM_WIDTH:2 * GM_WIDTH])
        mu = head_mean(v)
        yield
        d = v - mu
        var = head_mean(d * d)
        yield
        vn = d * lax.rsqrt(var + EPS) * lnw_ref[...] + lnb_ref[...]
        for k in range(GM_WIDTH // LANES):
            cols = slice(k * LANES, (k + 1) * LANES)
            mixed = jnp.dot(wcat_ref[k], pair_rhs(vn[:, cols]),
                            preferred_element_type=F32) + bs_ref[:, cols]
            mix_ref[rows, cols] = (_gelu(proj_ref[rows, cols]) * mixed).astype(BF16)
        yield

    def ssd(r):
        r0 = r * CHUNK
        rows = slice(r0, r0 + CHUNK)
        prows = slice(r0 + PAD_ROWS, r0 + PAD_ROWS + CHUNK)

        dt = _softplus(dtraw_ref[rows, :] + dtb_ref[...])
        dta = dt * (a_ref[...] * LOG2E)
        a_cs = _split_dot_left(tri_ref[...], dta, 3)
        yield

        xpad_ref[prows, :] = proj_ref[rows, XBC_COL:MAIN_COLS]
        for ct in range(CONV_CH // LANES):
            cols = slice(ct * LANES, (ct + 1) * LANES)
            acc = convb_ref[:, cols] + convw_ref[CONV - 1:CONV, cols] * xpad_ref[prows, cols]
            for k in range(1, CONV):
                acc = acc + convw_ref[CONV - 1 - k:CONV - k, cols] * xpad_ref[
                    r0 + PAD_ROWS - k:r0 + PAD_ROWS - k + CHUNK, cols]
            xc_ref[rows, cols] = _silu(acc)

        total = a_cs[CHUNK - 1:CHUNK, :]
        w_end = dt * jnp.exp2(total - a_cs)
        tr = jnp.where(lane < HEADS, a_cs - jnp.log(dt) * LOG2E, pltpu.roll(w_end, HEADS, 1)).T

        b_t, cb, y_off, state = [], [], [], []
        for g in range(GROUPS):
            b_g = xc_ref[rows, SSM_WIDTH + g * STATE:SSM_WIDTH + (g + 1) * STATE]
            c_g = xc_ref[rows, SSM_WIDTH + (GROUPS + g) * STATE:
                         SSM_WIDTH + (GROUPS + g + 1) * STATE].astype(BF16)
            cb.append(lax.dot_general(c_g, b_g.astype(BF16), (((1,), (1,)), ((), ())),
                                      preferred_element_type=F32))
            state.append(state_ref[g])
            y_off.append(jnp.dot(c_g, state[g].astype(BF16), preferred_element_type=F32))
            b_t.append(b_g.T)
        yield

        y_parts = []
        for g in range(GROUPS):
            y_tiles = []
            for p in range(HEADS_PER_GROUP // 2):
                pcols = slice(p * LANES, (p + 1) * LANES)
                scores, b_scaled, e_acs = [], [], []
                for q in range(2):
                    hd = g * HEADS_PER_GROUP + 2 * p + q
                    acs_b = jnp.broadcast_to(a_cs[:, hd:hd + 1], (CHUNK, LANES))
                    seg = acs_b - tr[hd:hd + 1, :]
                    scores.append(
                        (cb[g] * jnp.exp2(jnp.where(causal, seg, -jnp.inf))).astype(BF16))
                    b_scaled.append((b_t[g] * tr[HEADS + hd:HEADS + hd + 1, :]).astype(BF16))
                    e_acs.append(jnp.exp2(acs_b))
                tcol = g * (HEADS_PER_GROUP // 2) + p
                rhs = pair_rhs(xc_ref[rows, tcol * LANES:(tcol + 1) * LANES])
                lhs = jnp.concatenate([jnp.concatenate(scores, axis=1),
                                       jnp.concatenate(b_scaled, axis=1)], axis=0)
                res = jnp.dot(lhs, rhs, preferred_element_type=F32)
                decay = jnp.where(low_half, e_acs[0], e_acs[1])
                y_tiles.append(res[0:CHUNK, :] + y_off[g][:, pcols] * decay)
                state_ref[g, :, pcols] = (state[g][:, pcols] * decay[CHUNK - 1:CHUNK, :]
                                          + res[CHUNK:2 * CHUNK, :])
            y_parts.append(jnp.concatenate(y_tiles, axis=1))

        for g in range(GROUPS):
            gcols = slice(g * GROUP_WIDTH, (g + 1) * GROUP_WIDTH)
            zs = proj_ref[rows, Z_COL + g * GROUP_WIDTH:Z_COL + (g + 1) * GROUP_WIDTH]
            yg = (y_parts[g] + dskip_ref[:, gcols] * xc_ref[rows, gcols]) * _silu(zs)
            yn = yg * lax.rsqrt(jnp.mean(yg * yg, axis=-1, keepdims=True) + EPS)
            mix_ref[rows, GM_WIDTH + g * GROUP_WIDTH:GM_WIDTH + (g + 1) * GROUP_WIDTH] = (
                yn * nrmw_ref[:, gcols]).astype(BF16)
        yield

    def mixers(r, fill):
        chains = [gmlp(r), ssd(r)]
        while chains:
            for chain in list(chains):
                try:
                    next(chain)
                except StopIteration:
                    chains.remove(chain)
                    continue
                fill()

    half_rows = BLOCK // 2
    cons = [dict() for _ in range(2)]

    def out_proj_piece(hf):
        hrows = slice(hf * half_rows, (hf + 1) * half_rows)
        o = jnp.dot(mix_ref[hrows, :], wout_ref[...], preferred_element_type=F32)
        x1 = xres_ref[hrows, :] + _rms(o, npost_ref[...])
        cons[hf]["x1"] = x1
        cons[hf]["h2"] = _rms(x1, npre_ref[...]).astype(BF16)
        if hf == 1:
            xres_ref[...] = xa_ref[...]

    ff_pieces = list(range(0, D_FF, FF_PIECE))
    full = {}

    def up_piece(f0):
        h2 = jnp.concatenate([cons[0]["h2"], cons[1]["h2"]], axis=0)
        up = jnp.dot(h2, wup_ref[:, f0:f0 + FF_PIECE], preferred_element_type=F32)
        full[f0] = jnp.square(jnp.maximum(up, 0.0)).astype(BF16)

    def down_piece(f0):
        dn = jnp.dot(full[f0], wdown_ref[f0:f0 + FF_PIECE, :], preferred_element_type=F32)
        full["acc"] = dn if "acc" not in full else full["acc"] + dn
        if f0 == ff_pieces[-1]:
            for hf in range(2):
                hrows = slice(hf * half_rows, (hf + 1) * half_rows)
                out_ref[hrows, :] = cons[hf]["x1"] + _rms(full["acc"][hrows, :], nffn_ref[...])

    out_proj_piece(0)
    queue = [functools.partial(out_proj_piece, 1)]
    mid = len(ff_pieces) // 2
    for part in (ff_pieces[:mid], ff_pieces[mid:]):
        queue += [functools.partial(up_piece, f0) for f0 in part]
        queue += [functools.partial(down_piece, f0) for f0 in part]
    per_slot = [5, 4, 4, 4]
    assert sum(per_slot) == len(queue)

    upper = slice(half_rows, BLOCK)
    proj_queue = (in_proj_pieces(xa_ref[upper, :], upper)
                  + in_proj_pieces(xn_ref[...], slice(0, half_rows)))
    proj_per_slot = len(proj_queue) // n_chunks

    for r in range(n_chunks):
        mine = [queue.pop(0) for _ in range(per_slot[r])]
        nxt = [proj_queue.pop(0) for _ in range(proj_per_slot)]
        fillers = []
        while mine or nxt:
            if mine:
                fillers.append(mine.pop(0))
            if nxt:
                fillers.append(nxt.pop(0))

        def fill(fillers=fillers):
            if fillers:
                fillers.pop(0)()

        mixers(r, fill)
        while fillers:
            fill()

    xpad_ref[0:PAD_ROWS, :] = xpad_ref[BLOCK:BLOCK + PAD_ROWS, :]


def _layer(x2d, bsz, seq, norm_mix_pre, w_in, gm_ln_w, gm_ln_b, gm_w_s, gm_b_s, conv_w, conv_b,
           dt_bias, a_log, d_skip, ssm_norm_w, w_out, norm_mix_post, norm_ffn_pre, w_up, w_down,
           norm_ffn_post):
    t = bsz * seq
    assert seq % BLOCK == 0, (seq, BLOCK)
    n_blocks = t // BLOCK
    row = lambda p: p.reshape(1, -1).astype(F32)

    w_main = w_in[:, :MAIN_COLS].astype(BF16)
    w_dt = jnp.pad(w_in[:, MAIN_COLS:], ((0, 0), (0, LANES - HEADS))).astype(BF16)
    causal = jnp.tril(jnp.ones((CHUNK, CHUNK), dtype=bool))
    w_s = jnp.where(causal[None], gm_w_s, 0.0).astype(BF16)
    wcat = w_s.reshape(HEADS // 2, 2, CHUNK, CHUNK).transpose(0, 2, 1, 3).reshape(
        HEADS // 2, CHUNK, 2 * CHUNK)
    bs_e = jnp.repeat(gm_b_s.T.astype(F32), HEAD_DIM, axis=1)
    dtb = jnp.pad(dt_bias.astype(F32), (0, LANES - HEADS)).reshape(1, LANES)
    a_pad = jnp.pad(-jnp.exp(a_log.astype(F32)), (0, LANES - HEADS)).reshape(1, LANES)
    dskip_e = jnp.repeat(d_skip.astype(F32), HEAD_DIM).reshape(1, SSM_WIDTH)
    head_of_col = jnp.arange(MEAN_WIDTH) // HEAD_DIM
    mavg = jnp.where(head_of_col[:, None] == head_of_col[None, :], 1.0 / HEAD_DIM, 0.0).astype(BF16)
    tri = causal.astype(BF16)

    def const(shape):
        zeros = (0,) * len(shape)
        return pl.BlockSpec(shape, lambda i: zeros, pipeline_mode=pl.Buffered(1))

    last = n_blocks - 1
    last_half = 2 * n_blocks - 1
    in_specs = [
        pl.BlockSpec((BLOCK, D_MODEL), lambda i: (jnp.minimum(i, last), 0)),
        pl.BlockSpec((BLOCK // 2, D_MODEL),
                     lambda i: (jnp.minimum(2 * (i + 1), last_half), 0)),
        const((1, D_MODEL)),
        const((D_MODEL, MAIN_COLS)),
        const((D_MODEL, LANES)),
        const((1, GM_WIDTH)),
        const((1, GM_WIDTH)),
        const((HEADS // 2, CHUNK, 2 * CHUNK)),
        const((CHUNK, GM_WIDTH)),
        const((CONV, CONV_CH)),
        const((1, CONV_CH)),
        const((1, LANES)),
        const((1, LANES)),
        const((1, SSM_WIDTH)),
        const((1, SSM_WIDTH)),
        const((MEAN_WIDTH, MEAN_WIDTH)),
        const((CHUNK, CHUNK)),
        const((MIX_WIDTH, D_MODEL)),
        const((1, D_MODEL)),
        const((1, D_MODEL)),
        const((D_MODEL, D_FF)),
        const((D_FF, D_MODEL)),
        const((1, D_MODEL)),
    ]
    return pl.pallas_call(
        functools.partial(_layer_kernel, blocks_per_seq=seq // BLOCK, n_blocks=n_blocks),
        grid=(n_blocks + 1,),
        in_specs=in_specs,
        out_specs=pl.BlockSpec((BLOCK, D_MODEL), lambda i: (jnp.maximum(i - 1, 0), 0)),
        out_shape=jax.ShapeDtypeStruct((t, D_MODEL), F32),
        scratch_shapes=[
            pltpu.VMEM((BLOCK, MAIN_COLS), F32),
            pltpu.VMEM((BLOCK, LANES), F32),
            pltpu.VMEM((BLOCK, MIX_WIDTH), BF16),
            pltpu.VMEM((BLOCK + PAD_ROWS, CONV_CH), F32),
            pltpu.VMEM((GROUPS, STATE, GROUP_WIDTH), F32),
            pltpu.VMEM((BLOCK, CONV_CH), F32),
            pltpu.VMEM((BLOCK, D_MODEL), F32),
        ],
        compiler_params=pltpu.CompilerParams(
            dimension_semantics=("arbitrary",), vmem_limit_bytes=VMEM_LIMIT_BYTES),
        name="hybrid_layer",
    )(x2d, x2d, row(norm_mix_pre), w_main, w_dt, row(gm_ln_w), row(gm_ln_b), wcat, bs_e,
      conv_w.astype(F32), row(conv_b), dtb, a_pad, dskip_e, row(ssm_norm_w), mavg, tri,
      w_out.astype(BF16), row(norm_mix_post), row(norm_ffn_pre), w_up.astype(BF16),
      w_down.astype(BF16), row(norm_ffn_post))


def kernel(x, norm_mix_pre, w_in, gm_ln_w, gm_ln_b, gm_w_s, gm_b_s, conv_w, conv_b, dt_bias, a_log, d_skip, ssm_norm_w, w_out, norm_mix_post, norm_ffn_pre, w_up, w_down, norm_ffn_post):
    bsz, seq, d = x.shape
    depth = w_in.shape[0]
    x2d = x.reshape(bsz * seq, d)
    for i in range(depth):
        x2d = _layer(x2d, bsz, seq, norm_mix_pre[i], w_in[i], gm_ln_w[i], gm_ln_b[i], gm_w_s[i],
                     gm_b_s[i], conv_w[i], conv_b[i], dt_bias[i], a_log[i], d_skip[i],
                     ssm_norm_w[i], w_out[i], norm_mix_post[i], norm_ffn_pre[i], w_up[i],
                     w_down[i], norm_ffn_post[i])
    return x2d.reshape(bsz, seq, d)
```

```python
import functools
import math

import jax
import jax.numpy as jnp
from jax import lax
from jax.experimental import pallas as pl
from jax.experimental.pallas import tpu as pltpu

F32 = jnp.float32
BF16 = jnp.bfloat16

D_MODEL = 1024
GM_WIDTH = 512
HEAD_DIM = 64
HEADS = 8
CHUNK = 128
SSM_WIDTH = 512
GROUPS = 2
HEADS_PER_GROUP = HEADS // GROUPS
GROUP_WIDTH = HEADS_PER_GROUP * HEAD_DIM
STATE = 128
CONV = 4
CONV_CH = SSM_WIDTH + 2 * GROUPS * STATE
D_FF = 4 * D_MODEL
EPS = 1e-6
MAIN_COLS = 2 * GM_WIDTH + SSM_WIDTH + CONV_CH
Z_COL = 2 * GM_WIDTH
XBC_COL = Z_COL + SSM_WIDTH
MIX_WIDTH = GM_WIDTH + SSM_WIDTH
LANES = 128
MEAN_WIDTH = 256
PAD_ROWS = 8
LOG2E = 1.4426950408889634

VMEM_LIMIT_BYTES = 60 * 1024 * 1024

BLOCK = 512
PROJ_BLOCK = 512
FF_PIECE = 512


def _rms(x, w):
    return x * lax.rsqrt(jnp.mean(x * x, axis=-1, keepdims=True) + EPS) * w


def _split_dot_left(m, a, passes):
    pieces = []
    rem = a
    for _ in range(passes):
        piece = rem.astype(BF16)
        pieces.append(piece)
        rem = rem - piece.astype(F32)
    d = jnp.dot(m, jnp.concatenate(pieces, axis=1), preferred_element_type=F32)
    width = a.shape[1]
    acc = d[:, 0:width]
    for k in range(1, passes):
        acc = acc + d[:, k * width:(k + 1) * width]
    return acc


_GELU_A = -2.0 * math.sqrt(2.0 / math.pi) * LOG2E
_GELU_B = _GELU_A * 0.044715


def _gelu(x):
    return x / (1.0 + jnp.exp2(x * (_GELU_A + _GELU_B * (x * x))))


def _silu(x):
    return x / (1.0 + jnp.exp2(x * (-LOG2E)))


def _softplus(x):
    return jnp.maximum(x, 0.0) + jnp.log1p(jnp.exp(-jnp.abs(x)))


def _layer_kernel(
        xa_ref, xb_ref, xn_ref, nmix_ref, win_ref, wdt_ref, lnw_ref, lnb_ref, wcat_ref, bs_ref,
        convw_ref, convb_ref, dtb_ref, a_ref, dskip_ref, nrmw_ref, mavg_ref, tri_ref,
        wout_ref, npost_ref, npre_ref, wup_ref, wdown_ref, nffn_ref,
        out_ref,
        proj_ref, dtraw_ref, mix_ref, xpad_ref, state_ref, xc_ref, pnext_ref, dnext_ref,
        *, blocks_per_seq, n_blocks):
    i = pl.program_id(0)
    n_chunks = BLOCK // CHUNK
    first_of_seq = lax.rem(jnp.minimum(i, n_blocks - 1), blocks_per_seq) == 0

    def in_proj_pieces(xv, dst):
        hn = _rms(xv, nmix_ref[...]).astype(BF16)

        def main(n0):
            proj_ref[dst, n0:n0 + PROJ_BLOCK] = jnp.dot(
                hn, win_ref[:, n0:n0 + PROJ_BLOCK], preferred_element_type=F32)

        def dt():
            dtraw_ref[dst, :] = jnp.dot(hn, wdt_ref[...], preferred_element_type=F32)

        return [functools.partial(main, n0) for n0 in range(0, MAIN_COLS, PROJ_BLOCK)] + [dt]

    def in_proj_block_pieces():
        half = BLOCK // 2
        xv = jnp.concatenate([xa_ref[half:BLOCK, :], xn_ref[...]], axis=0)
        hn = _rms(xv, nmix_ref[...]).astype(BF16)

        def main(n0):
            res = jnp.dot(hn, win_ref[:, n0:n0 + PROJ_BLOCK], preferred_element_type=F32)
            proj_ref[half:BLOCK, n0:n0 + PROJ_BLOCK] = res[0:half, :]
            pnext_ref[:, n0:n0 + PROJ_BLOCK] = res[half:BLOCK, :]

        def dt():
            res = jnp.dot(hn, wdt_ref[...], preferred_element_type=F32)
            dtraw_ref[half:BLOCK, :] = res[0:half, :]
            dnext_ref[...] = res[half:BLOCK, :]

        return [functools.partial(main, n0) for n0 in range(0, MAIN_COLS, PROJ_BLOCK)] + [dt]

    @pl.when(i == 0)
    def _():
        mix_ref[...] = jnp.zeros(mix_ref.shape, BF16)
        for piece in in_proj_pieces(xa_ref[0:BLOCK // 2, :], slice(0, BLOCK // 2)):
            piece()

    @pl.when(first_of_seq)
    def _():
        xpad_ref[0:PAD_ROWS, :] = jnp.zeros((PAD_ROWS, CONV_CH), F32)
        state_ref[...] = jnp.zeros(state_ref.shape, F32)

    lane = lax.broadcasted_iota(jnp.int32, (CHUNK, LANES), 1)
    row = lax.broadcasted_iota(jnp.int32, (CHUNK, LANES), 0)
    causal = row >= lane
    low_half = lane < HEAD_DIM

    def pair_rhs(tile):
        return jnp.concatenate([jnp.where(low_half, tile, 0.0).astype(BF16),
                                jnp.where(low_half, 0.0, tile).astype(BF16)], axis=0)

    def gmlp(r):
        rows = slice(r * CHUNK, (r + 1) * CHUNK)

        def head_mean(a):
            ab = a.astype(BF16)
            col_blocks = range(0, GM_WIDTH, MEAN_WIDTH)
            stacked = jnp.concatenate([ab[:, c0:c0 + MEAN_WIDTH] for c0 in col_blocks], axis=0)
            m = jnp.dot(stacked, mavg_ref[...], preferred_element_type=F32)
            return jnp.concatenate(
                [m[k * CHUNK:(k + 1) * CHUNK, :] for k in range(len(col_blocks))], axis=1)

        v = _gelu(proj_ref[rows, GM_WIDTH:2 * GM_WIDTH])
        mu = head_mean(v)
        yield
        d = v - mu
        var = head_mean(d * d)
        yield
        vn = d * lax.rsqrt(var + EPS) * lnw_ref[...] + lnb_ref[...]
        for k in range(GM_WIDTH // LANES):
            cols = slice(k * LANES, (k + 1) * LANES)
            mixed = jnp.dot(wcat_ref[k], pair_rhs(vn[:, cols]),
                            preferred_element_type=F32) + bs_ref[:, cols]
            mix_ref[rows, cols] = (_gelu(proj_ref[rows, cols]) * mixed).astype(BF16)
        yield

    def ssd(r):
        r0 = r * CHUNK
        rows = slice(r0, r0 + CHUNK)
        prows = slice(r0 + PAD_ROWS, r0 + PAD_ROWS + CHUNK)

        dt = _softplus(dtraw_ref[rows, :] + dtb_ref[...])
        dta = dt * (a_ref[...] * LOG2E)
        a_cs = _split_dot_left(tri_ref[...], dta, 3)
        yield

        xpad_ref[prows, :] = proj_ref[rows, XBC_COL:MAIN_COLS]
        for ct in range(CONV_CH // LANES):
            cols = slice(ct * LANES, (ct + 1) * LANES)
            acc = convb_ref[:, cols] + convw_ref[CONV - 1:CONV, cols] * xpad_ref[prows, cols]
            for k in range(1, CONV):
                acc = acc + convw_ref[CONV - 1 - k:CONV - k, cols] * xpad_ref[
                    r0 + PAD_ROWS - k:r0 + PAD_ROWS - k + CHUNK, cols]
            xc_ref[rows, cols] = _silu(acc)

        total = a_cs[CHUNK - 1:CHUNK, :]
        w_end = dt * jnp.exp2(total - a_cs)
        tr = jnp.where(lane < HEADS, a_cs - jnp.log(dt) * LOG2E, pltpu.roll(w_end, HEADS, 1)).T

        b_t, cb, y_off, state = [], [], [], []
        for g in range(GROUPS):
            b_g = xc_ref[rows, SSM_WIDTH + g * STATE:SSM_WIDTH + (g + 1) * STATE]
            c_g = xc_ref[rows, SSM_WIDTH + (GROUPS + g) * STATE:
                         SSM_WIDTH + (GROUPS + g + 1) * STATE].astype(BF16)
            cb.append(lax.dot_general(c_g, b_g.astype(BF16), (((1,), (1,)), ((), ())),
                                      preferred_element_type=F32))
            state.append(state_ref[g])
            y_off.append(jnp.dot(c_g, state[g].astype(BF16), preferred_element_type=F32))
            b_t.append(b_g.T)
        yield

        y_parts = []
        for g in range(GROUPS):
            y_tiles = []
            for p in range(HEADS_PER_GROUP // 2):
                pcols = slice(p * LANES, (p + 1) * LANES)
                scores, b_scaled, e_acs = [], [], []
                for q in range(2):
                    hd = g * HEADS_PER_GROUP + 2 * p + q
                    acs_b = jnp.broadcast_to(a_cs[:, hd:hd + 1], (CHUNK, LANES))
                    seg = acs_b - tr[hd:hd + 1, :]
                    scores.append(
                        (cb[g] * jnp.exp2(jnp.where(causal, seg, -jnp.inf))).astype(BF16))
                    b_scaled.append((b_t[g] * tr[HEADS + hd:HEADS + hd + 1, :]).astype(BF16))
                    e_acs.append(jnp.exp2(acs_b))
                tcol = g * (HEADS_PER_GROUP // 2) + p
                rhs = pair_rhs(xc_ref[rows, tcol * LANES:(tcol + 1) * LANES])
                lhs = jnp.concatenate([jnp.concatenate(scores, axis=1),
                                       jnp.concatenate(b_scaled, axis=1)], axis=0)
                res = jnp.dot(lhs, rhs, preferred_element_type=F32)
                decay = jnp.where(low_half, e_acs[0], e_acs[1])
                y_tiles.append(res[0:CHUNK, :] + y_off[g][:, pcols] * decay)
                state_ref[g, :, pcols] = (state[g][:, pcols] * decay[CHUNK - 1:CHUNK, :]
                                          + res[CHUNK:2 * CHUNK, :])
            y_parts.append(jnp.concatenate(y_tiles, axis=1))

        for g in range(GROUPS):
            gcols = slice(g * GROUP_WIDTH, (g + 1) * GROUP_WIDTH)
            zs = proj_ref[rows, Z_COL + g * GROUP_WIDTH:Z_COL + (g + 1) * GROUP_WIDTH]
            yg = (y_parts[g] + dskip_ref[:, gcols] * xc_ref[rows, gcols]) * _silu(zs)
            yn = yg * lax.rsqrt(jnp.mean(yg * yg, axis=-1, keepdims=True) + EPS)
            mix_ref[rows, GM_WIDTH + g * GROUP_WIDTH:GM_WIDTH + (g + 1) * GROUP_WIDTH] = (
                yn * nrmw_ref[:, gcols]).astype(BF16)
        yield

    def mixers(r, fill):
        chains = [gmlp(r), ssd(r)]
        while chains:
            for chain in list(chains):
                try:
                    next(chain)
                except StopIteration:
                    chains.remove(chain)
                    continue
                fill()

    half_rows = BLOCK // 2
    cons = [dict() for _ in range(2)]

    def out_proj_piece(hf):
        hrows = slice(hf * half_rows, (hf + 1) * half_rows)
        o = jnp.dot(mix_ref[hrows, :], wout_ref[...], preferred_element_type=F32)
        x1 = xb_ref[hrows, :] + _rms(o, npost_ref[...])
        cons[hf]["x1"] = x1
        cons[hf]["h2"] = _rms(x1, npre_ref[...]).astype(BF16)

    ff_pieces = list(range(0, D_FF, FF_PIECE))
    full = {}

    def up_piece(f0):
        h2 = jnp.concatenate([cons[0]["h2"], cons[1]["h2"]], axis=0)
        up = jnp.dot(h2, wup_ref[:, f0:f0 + FF_PIECE], preferred_element_type=F32)
        full[f0] = jnp.square(jnp.maximum(up, 0.0)).astype(BF16)

    def down_piece(f0):
        dn = jnp.dot(full[f0], wdown_ref[f0:f0 + FF_PIECE, :], preferred_element_type=F32)
        full["acc"] = dn if "acc" not in full else full["acc"] + dn
        if f0 == ff_pieces[-1]:
            for hf in range(2):
                hrows = slice(hf * half_rows, (hf + 1) * half_rows)
                out_ref[hrows, :] = cons[hf]["x1"] + _rms(full["acc"][hrows, :], nffn_ref[...])

    out_proj_piece(0)
    queue = [functools.partial(out_proj_piece, 1)]
    mid = len(ff_pieces) // 2
    for part in (ff_pieces[:mid], ff_pieces[mid:]):
        queue += [functools.partial(up_piece, f0) for f0 in part]
        queue += [functools.partial(down_piece, f0) for f0 in part]
    per_slot = [3, 3, 5, 6]
    assert sum(per_slot) == len(queue)

    proj_queue = in_proj_block_pieces()
    proj_per_slot = [len(proj_queue) // 2, len(proj_queue) // 2, 0, 0]

    for r in range(n_chunks):
        mine = [queue.pop(0) for _ in range(per_slot[r])]
        nxt = [proj_queue.pop(0) for _ in range(proj_per_slot[r])]
        fillers = []
        while mine or nxt:
            if mine:
                fillers.append(mine.pop(0))
            if nxt:
                fillers.append(nxt.pop(0))

        def fill(fillers=fillers):
            if fillers:
                fillers.pop(0)()

        mixers(r, fill)
        while fillers:
            fill()
        if r == 1:
            proj_ref[0:half_rows, :] = pnext_ref[...]
            dtraw_ref[0:half_rows, :] = dnext_ref[...]

    xpad_ref[0:PAD_ROWS, :] = xpad_ref[BLOCK:BLOCK + PAD_ROWS, :]


def _layer(x2d, bsz, seq, norm_mix_pre, w_in, gm_ln_w, gm_ln_b, gm_w_s, gm_b_s, conv_w, conv_b,
           dt_bias, a_log, d_skip, ssm_norm_w, w_out, norm_mix_post, norm_ffn_pre, w_up, w_down,
           norm_ffn_post):
    t = bsz * seq
    assert seq % BLOCK == 0, (seq, BLOCK)
    n_blocks = t // BLOCK
    row = lambda p: p.reshape(1, -1).astype(F32)

    w_main = w_in[:, :MAIN_COLS].astype(BF16)
    w_dt = jnp.pad(w_in[:, MAIN_COLS:], ((0, 0), (0, LANES - HEADS))).astype(BF16)
    causal = jnp.tril(jnp.ones((CHUNK, CHUNK), dtype=bool))
    w_s = jnp.where(causal[None], gm_w_s, 0.0).astype(BF16)
    wcat = w_s.reshape(HEADS // 2, 2, CHUNK, CHUNK).transpose(0, 2, 1, 3).reshape(
        HEADS // 2, CHUNK, 2 * CHUNK)
    bs_e = jnp.repeat(gm_b_s.T.astype(F32), HEAD_DIM, axis=1)
    dtb = jnp.pad(dt_bias.astype(F32), (0, LANES - HEADS)).reshape(1, LANES)
    a_pad = jnp.pad(-jnp.exp(a_log.astype(F32)), (0, LANES - HEADS)).reshape(1, LANES)
    dskip_e = jnp.repeat(d_skip.astype(F32), HEAD_DIM).reshape(1, SSM_WIDTH)
    head_of_col = jnp.arange(MEAN_WIDTH) // HEAD_DIM
    mavg = jnp.where(head_of_col[:, None] == head_of_col[None, :], 1.0 / HEAD_DIM, 0.0).astype(BF16)
    tri = causal.astype(BF16)

    def const(shape):
        zeros = (0,) * len(shape)
        return pl.BlockSpec(shape, lambda i: zeros, pipeline_mode=pl.Buffered(1))

    last = n_blocks - 1
    last_half = 2 * n_blocks - 1
    in_specs = [
        pl.BlockSpec((BLOCK, D_MODEL), lambda i: (jnp.minimum(i, last), 0)),
        pl.BlockSpec((BLOCK, D_MODEL), lambda i: (jnp.maximum(i - 1, 0), 0)),
        pl.BlockSpec((BLOCK // 2, D_MODEL),
                     lambda i: (jnp.minimum(2 * (i + 1), last_half), 0)),
        const((1, D_MODEL)),
        const((D_MODEL, MAIN_COLS)),
        const((D_MODEL, LANES)),
        const((1, GM_WIDTH)),
        const((1, GM_WIDTH)),
        const((HEADS // 2, CHUNK, 2 * CHUNK)),
        const((CHUNK, GM_WIDTH)),
        const((CONV, CONV_CH)),
        const((1, CONV_CH)),
        const((1, LANES)),
        const((1, LANES)),
        const((1, SSM_WIDTH)),
        const((1, SSM_WIDTH)),
        const((MEAN_WIDTH, MEAN_WIDTH)),
        const((CHUNK, CHUNK)),
        const((MIX_WIDTH, D_MODEL)),
        const((1, D_MODEL)),
        const((1, D_MODEL)),
        const((D_MODEL, D_FF)),
        const((D_FF, D_MODEL)),
        const((1, D_MODEL)),
    ]
    return pl.pallas_call(
        functools.partial(_layer_kernel, blocks_per_seq=seq // BLOCK, n_blocks=n_blocks),
        grid=(n_blocks + 1,),
        in_specs=in_specs,
        out_specs=pl.BlockSpec((BLOCK, D_MODEL), lambda i: (jnp.maximum(i - 1, 0), 0)),
        out_shape=jax.ShapeDtypeStruct((t, D_MODEL), F32),
        scratch_shapes=[
            pltpu.VMEM((BLOCK, MAIN_COLS), F32),
            pltpu.VMEM((BLOCK, LANES), F32),
            pltpu.VMEM((BLOCK, MIX_WIDTH), BF16),
            pltpu.VMEM((BLOCK + PAD_ROWS, CONV_CH), F32),
            pltpu.VMEM((GROUPS, STATE, GROUP_WIDTH), F32),
            pltpu.VMEM((BLOCK, CONV_CH), F32),
            pltpu.VMEM((BLOCK // 2, MAIN_COLS), F32),
            pltpu.VMEM((BLOCK // 2, LANES), F32),
        ],
        compiler_params=pltpu.CompilerParams(
            dimension_semantics=("arbitrary",), vmem_limit_bytes=VMEM_LIMIT_BYTES),
        name="hybrid_layer",
    )(x2d, x2d, x2d, row(norm_mix_pre), w_main, w_dt, row(gm_ln_w), row(gm_ln_b), wcat, bs_e,
      conv_w.astype(F32), row(conv_b), dtb, a_pad, dskip_e, row(ssm_norm_w), mavg, tri,
      w_out.astype(BF16), row(norm_mix_post), row(norm_ffn_pre), w_up.astype(BF16),
      w_down.astype(BF16), row(norm_ffn_post))


def kernel(x, norm_mix_pre, w_in, gm_ln_w, gm_ln_b, gm_w_s, gm_b_s, conv_w, conv_b, dt_bias, a_log, d_skip, ssm_norm_w, w_out, norm_mix_post, norm_ffn_pre, w_up, w_down, norm_ffn_post):
    bsz, seq, d = x.shape
    depth = w_in.shape[0]
    x2d = x.reshape(bsz * seq, d)
    for i in range(depth):
        x2d = _layer(x2d, bsz, seq, norm_mix_pre[i], w_in[i], gm_ln_w[i], gm_ln_b[i], gm_w_s[i],
                     gm_b_s[i], conv_w[i], conv_b[i], dt_bias[i], a_log[i], d_skip[i],
                     ssm_norm_w[i], w_out[i], norm_mix_post[i], norm_ffn_pre[i], w_up[i],
                     w_down[i], norm_ffn_post[i])
    return x2d.reshape(bsz, seq, d)
```

```python
import functools
import math

import jax
import jax.numpy as jnp
from jax import lax
from jax.experimental import pallas as pl
from jax.experimental.pallas import tpu as pltpu

F32 = jnp.float32
BF16 = jnp.bfloat16

D_MODEL = 1024
GM_WIDTH = 512
HEAD_DIM = 64
HEADS = 8
CHUNK = 128
SSM_WIDTH = 512
GROUPS = 2
HEADS_PER_GROUP = HEADS // GROUPS
GROUP_WIDTH = HEADS_PER_GROUP * HEAD_DIM
STATE = 128
CONV = 4
CONV_CH = SSM_WIDTH + 2 * GROUPS * STATE
D_FF = 4 * D_MODEL
EPS = 1e-6
MAIN_COLS = 2 * GM_WIDTH + SSM_WIDTH + CONV_CH
Z_COL = 2 * GM_WIDTH
XBC_COL = Z_COL + SSM_WIDTH
MIX_WIDTH = GM_WIDTH + SSM_WIDTH
LANES = 128
MEAN_WIDTH = 256
PAD_ROWS = 8
LOG2E = 1.4426950408889634

VMEM_LIMIT_BYTES = 60 * 1024 * 1024

BLOCK = 512
PROJ_BLOCK = 512
FF_PIECE = 512


def _rms(x, w):
    return x * lax.rsqrt(jnp.mean(x * x, axis=-1, keepdims=True) + EPS) * w


def _split_dot_left(m, a, passes):
    pieces = []
    rem = a
    for _ in range(passes):
        piece = rem.astype(BF16)
        pieces.append(piece)
        rem = rem - piece.astype(F32)
    d = jnp.dot(m, jnp.concatenate(pieces, axis=1), preferred_element_type=F32)
    width = a.shape[1]
    acc = d[:, 0:width]
    for k in range(1, passes):
        acc = acc + d[:, k * width:(k + 1) * width]
    return acc


_GELU_A = -2.0 * math.sqrt(2.0 / math.pi) * LOG2E
_GELU_B = _GELU_A * 0.044715


def _gelu(x):
    return x / (1.0 + jnp.exp2(x * (_GELU_A + _GELU_B * (x * x))))


def _silu(x):
    return x / (1.0 + jnp.exp2(x * (-LOG2E)))


def _softplus(x):
    return jnp.maximum(x, 0.0) + jnp.log1p(jnp.exp(-jnp.abs(x)))


def _layer_kernel(
        xa_ref, xb_ref, xn_ref, nmix_ref, win_ref, wdt_ref, lnw_ref, lnb_ref, wcat_ref, bs_ref,
        convw_ref, convb_ref, dtb_ref, a_ref, dskip_ref, nrmw_ref, mavg_ref, tri_ref,
        wout_ref, npost_ref, npre_ref, wup_ref, wdown_ref, nffn_ref,
        out_ref,
        proj_ref, dtraw_ref, mix_ref, xpad_ref, state_ref, xc_ref,
        *, blocks_per_seq, n_blocks):
    i = pl.program_id(0)
    n_chunks = BLOCK // CHUNK
    first_of_seq = lax.rem(jnp.minimum(i, n_blocks - 1), blocks_per_seq) == 0

    def in_proj_pieces(xv, dst):
        hn = _rms(xv, nmix_ref[...]).astype(BF16)

        def main(n0):
            proj_ref[dst, n0:n0 + PROJ_BLOCK] = jnp.dot(
                hn, win_ref[:, n0:n0 + PROJ_BLOCK], preferred_element_type=F32)

        def dt():
            dtraw_ref[dst, :] = jnp.dot(hn, wdt_ref[...], preferred_element_type=F32)

        return [functools.partial(main, n0) for n0 in range(0, MAIN_COLS, PROJ_BLOCK)] + [dt]

    @pl.when(i == 0)
    def _():
        mix_ref[...] = jnp.zeros(mix_ref.shape, BF16)
        for piece in in_proj_pieces(xa_ref[0:BLOCK // 2, :], slice(0, BLOCK // 2)):
            piece()

    @pl.when(first_of_seq)
    def _():
        xpad_ref[0:PAD_ROWS, :] = jnp.zeros((PAD_ROWS, CONV_CH), F32)
        state_ref[...] = jnp.zeros(state_ref.shape, F32)

    lane = lax.broadcasted_iota(jnp.int32, (CHUNK, LANES), 1)
    row = lax.broadcasted_iota(jnp.int32, (CHUNK, LANES), 0)
    causal = row >= lane
    low_half = lane < HEAD_DIM

    def pair_rhs(tile):
        return jnp.concatenate([jnp.where(low_half, tile, 0.0).astype(BF16),
                                jnp.where(low_half, 0.0, tile).astype(BF16)], axis=0)

    def gmlp(r):
        rows = slice(r * CHUNK, (r + 1) * CHUNK)

        def head_mean(a):
            ab = a.astype(BF16)
            col_blocks = range(0, GM_WIDTH, MEAN_WIDTH)
            stacked = jnp.concatenate([ab[:, c0:c0 + MEAN_WIDTH] for c0 in col_blocks], axis=0)
            m = jnp.dot(stacked, mavg_ref[...], preferred_element_type=F32)
            return jnp.concatenate(
                [m[k * CHUNK:(k + 1) * CHUNK, :] for k in range(len(col_blocks))], axis=1)

        v = _gelu(proj_ref[rows, GM_WIDTH:2 * GM_WIDTH])
        mu = head_mean(v)
        yield
        d = v - mu
        var = head_mean(d * d)
        yield
        vn = d * lax.rsqrt(var + EPS) * lnw_ref[...] + lnb_ref[...]
        for k in range(GM_WIDTH // LANES):
            cols = slice(k * LANES, (k + 1) * LANES)
            mixed = jnp.dot(wcat_ref[k], pair_rhs(vn[:, cols]),
                            preferred_element_type=F32) + bs_ref[:, cols]
            mix_ref[rows, cols] = (_gelu(proj_ref[rows, cols]) * mixed).astype(BF16)
        yield

    def ssd(r):
        r0 = r * CHUNK
        rows = slice(r0, r0 + CHUNK)
        prows = slice(r0 + PAD_ROWS, r0 + PAD_ROWS + CHUNK)

        dt = _softplus(dtraw_ref[rows, :] + dtb_ref[...])
        dta = dt * (a_ref[...] * LOG2E)
        a_cs = _split_dot_left(tri_ref[...], dta, 3)
        yield

        xpad_ref[prows, :] = proj_ref[rows, XBC_COL:MAIN_COLS]
        for ct in range(CONV_CH // LANES):
            cols = slice(ct * LANES, (ct + 1) * LANES)
            acc = convb_ref[:, cols] + convw_ref[CONV - 1:CONV, cols] * xpad_ref[prows, cols]
            for k in range(1, CONV):
                acc = acc + convw_ref[CONV - 1 - k:CONV - k, cols] * xpad_ref[
                    r0 + PAD_ROWS - k:r0 + PAD_ROWS - k + CHUNK, cols]
            xc_ref[rows, cols] = _silu(acc)

        total = a_cs[CHUNK - 1:CHUNK, :]
        w_end = dt * jnp.exp2(total - a_cs)
        tr = jnp.where(lane < HEADS, a_cs - jnp.log(dt) * LOG2E, pltpu.roll(w_end, HEADS, 1)).T

        b_t, cb, y_off, state = [], [], [], []
        for g in range(GROUPS):
            b_g = xc_ref[rows, SSM_WIDTH + g * STATE:SSM_WIDTH + (g + 1) * STATE]
            c_g = xc_ref[rows, SSM_WIDTH + (GROUPS + g) * STATE:
                         SSM_WIDTH + (GROUPS + g + 1) * STATE].astype(BF16)
            cb.append(lax.dot_general(c_g, b_g.astype(BF16), (((1,), (1,)), ((), ())),
                                      preferred_element_type=F32))
            state.append(state_ref[g])
            y_off.append(jnp.dot(c_g, state[g].astype(BF16), preferred_element_type=F32))
            b_t.append(b_g.T)
        yield

        y_parts = []
        for g in range(GROUPS):
            y_tiles = []
            for p in range(HEADS_PER_GROUP // 2):
                pcols = slice(p * LANES, (p + 1) * LANES)
                scores, b_scaled, e_acs = [], [], []
                for q in range(2):
                    hd = g * HEADS_PER_GROUP + 2 * p + q
                    acs_b = jnp.broadcast_to(a_cs[:, hd:hd + 1], (CHUNK, LANES))
                    seg = acs_b - tr[hd:hd + 1, :]
                    scores.append(
                        (cb[g] * jnp.exp2(jnp.where(causal, seg, -jnp.inf))).astype(BF16))
                    b_scaled.append((b_t[g] * tr[HEADS + hd:HEADS + hd + 1, :]).astype(BF16))
                    e_acs.append(jnp.exp2(acs_b))
                tcol = g * (HEADS_PER_GROUP // 2) + p
                rhs = pair_rhs(xc_ref[rows, tcol * LANES:(tcol + 1) * LANES])
                lhs = jnp.concatenate([jnp.concatenate(scores, axis=1),
                                       jnp.concatenate(b_scaled, axis=1)], axis=0)
                res = jnp.dot(lhs, rhs, preferred_element_type=F32)
                decay = jnp.where(low_half, e_acs[0], e_acs[1])
                y_tiles.append(res[0:CHUNK, :] + y_off[g][:, pcols] * decay)
                state_ref[g, :, pcols] = (state[g][:, pcols] * decay[CHUNK - 1:CHUNK, :]
                                          + res[CHUNK:2 * CHUNK, :])
            y_parts.append(jnp.concatenate(y_tiles, axis=1))

        for g in range(GROUPS):
            gcols = slice(g * GROUP_WIDTH, (g + 1) * GROUP_WIDTH)
            zs = proj_ref[rows, Z_COL + g * GROUP_WIDTH:Z_COL + (g + 1) * GROUP_WIDTH]
            yg = (y_parts[g] + dskip_ref[:, gcols] * xc_ref[rows, gcols]) * _silu(zs)
            yn = yg * lax.rsqrt(jnp.mean(yg * yg, axis=-1, keepdims=True) + EPS)
            mix_ref[rows, GM_WIDTH + g * GROUP_WIDTH:GM_WIDTH + (g + 1) * GROUP_WIDTH] = (
                yn * nrmw_ref[:, gcols]).astype(BF16)
        yield

    def mixers(r, fill):
        chains = [gmlp(r), ssd(r)]
        while chains:
            for chain in list(chains):
                try:
                    next(chain)
                except StopIteration:
                    chains.remove(chain)
                    continue
                fill()

    half_rows = BLOCK // 2
    cons = [dict() for _ in range(2)]

    def out_proj_piece(hf):
        hrows = slice(hf * half_rows, (hf + 1) * half_rows)
        o = jnp.dot(mix_ref[hrows, :], wout_ref[...], preferred_element_type=F32)
        x1 = xb_ref[hrows, :] + _rms(o, npost_ref[...])
        cons[hf]["x1"] = x1
        cons[hf]["h2"] = _rms(x1, npre_ref[...]).astype(BF16)

    ff_pieces = list(range(0, D_FF, FF_PIECE))
    full = {}

    def up_piece(f0):
        h2 = jnp.concatenate([cons[0]["h2"], cons[1]["h2"]], axis=0)
        up = jnp.dot(h2, wup_ref[:, f0:f0 + FF_PIECE], preferred_element_type=F32)
        full[f0] = jnp.square(jnp.maximum(up, 0.0)).astype(BF16)

    def down_piece(f0):
        dn = jnp.dot(full[f0], wdown_ref[f0:f0 + FF_PIECE, :], preferred_element_type=F32)
        full["acc"] = dn if "acc" not in full else full["acc"] + dn
        if f0 == ff_pieces[-1]:
            for hf in range(2):
                hrows = slice(hf * half_rows, (hf + 1) * half_rows)
                out_ref[hrows, :] = cons[hf]["x1"] + _rms(full["acc"][hrows, :], nffn_ref[...])

    out_proj_piece(0)
    queue = [functools.partial(out_proj_piece, 1)]
    mid = len(ff_pieces) // 2
    for part in (ff_pieces[:mid], ff_pieces[mid:]):
        queue += [functools.partial(up_piece, f0) for f0 in part]
        queue += [functools.partial(down_piece, f0) for f0 in part]
    per_slot = [5, 4, 4, 4]
    assert sum(per_slot) == len(queue)

    upper = slice(half_rows, BLOCK)
    proj_queue = (in_proj_pieces(xa_ref[upper, :], upper)
                  + in_proj_pieces(xn_ref[...], slice(0, half_rows)))
    proj_per_slot = len(proj_queue) // n_chunks

    for r in range(n_chunks):
        mine = [queue.pop(0) for _ in range(per_slot[r])]
        nxt = [proj_queue.pop(0) for _ in range(proj_per_slot)]
        fillers = []
        while mine or nxt:
            if nxt:
                fillers.append(nxt.pop(0))
            if mine:
                fillers.append(mine.pop(0))

        def fill(fillers=fillers):
            if fillers:
                fillers.pop(0)()

        mixers(r, fill)
        while fillers:
            fill()

    xpad_ref[0:PAD_ROWS, :] = xpad_ref[BLOCK:BLOCK + PAD_ROWS, :]


def _layer(x2d, bsz, seq, norm_mix_pre, w_in, gm_ln_w, gm_ln_b, gm_w_s, gm_b_s, conv_w, conv_b,
           dt_bias, a_log, d_skip, ssm_norm_w, w_out, norm_mix_post, norm_ffn_pre, w_up, w_down,
           norm_ffn_post):
    t = bsz * seq
    assert seq % BLOCK == 0, (seq, BLOCK)
    n_blocks = t // BLOCK
    row = lambda p: p.reshape(1, -1).astype(F32)

    w_main = w_in[:, :MAIN_COLS].astype(BF16)
    w_dt = jnp.pad(w_in[:, MAIN_COLS:], ((0, 0), (0, LANES - HEADS))).astype(BF16)
    causal = jnp.tril(jnp.ones((CHUNK, CHUNK), dtype=bool))
    w_s = jnp.where(causal[None], gm_w_s, 0.0).astype(BF16)
    wcat = w_s.reshape(HEADS // 2, 2, CHUNK, CHUNK).transpose(0, 2, 1, 3).reshape(
        HEADS // 2, CHUNK, 2 * CHUNK)
    bs_e = jnp.repeat(gm_b_s.T.astype(F32), HEAD_DIM, axis=1)
    dtb = jnp.pad(dt_bias.astype(F32), (0, LANES - HEADS)).reshape(1, LANES)
    a_pad = jnp.pad(-jnp.exp(a_log.astype(F32)), (0, LANES - HEADS)).reshape(1, LANES)
    dskip_e = jnp.repeat(d_skip.astype(F32), HEAD_DIM).reshape(1, SSM_WIDTH)
    head_of_col = jnp.arange(MEAN_WIDTH) // HEAD_DIM
    mavg = jnp.where(head_of_col[:, None] == head_of_col[None, :], 1.0 / HEAD_DIM, 0.0).astype(BF16)
    tri = causal.astype(BF16)

    def const(shape):
        zeros = (0,) * len(shape)
        return pl.BlockSpec(shape, lambda i: zeros, pipeline_mode=pl.Buffered(1))

    last = n_blocks - 1
    last_half = 2 * n_blocks - 1
    in_specs = [
        pl.BlockSpec((BLOCK, D_MODEL), lambda i: (jnp.minimum(i, last), 0)),
        pl.BlockSpec((BLOCK, D_MODEL), lambda i: (jnp.maximum(i - 1, 0), 0)),
        pl.BlockSpec((BLOCK // 2, D_MODEL),
                     lambda i: (jnp.minimum(2 * (i + 1), last_half), 0)),
        const((1, D_MODEL)),
        const((D_MODEL, MAIN_COLS)),
        const((D_MODEL, LANES)),
        const((1, GM_WIDTH)),
        const((1, GM_WIDTH)),
        const((HEADS // 2, CHUNK, 2 * CHUNK)),
        const((CHUNK, GM_WIDTH)),
        const((CONV, CONV_CH)),
        const((1, CONV_CH)),
        const((1, LANES)),
        const((1, LANES)),
        const((1, SSM_WIDTH)),
        const((1, SSM_WIDTH)),
        const((MEAN_WIDTH, MEAN_WIDTH)),
        const((CHUNK, CHUNK)),
        const((MIX_WIDTH, D_MODEL)),
        const((1, D_MODEL)),
        const((1, D_MODEL)),
        const((D_MODEL, D_FF)),
        const((D_FF, D_MODEL)),
        const((1, D_MODEL)),
    ]
    return pl.pallas_call(
        functools.partial(_layer_kernel, blocks_per_seq=seq // BLOCK, n_blocks=n_blocks),
        grid=(n_blocks + 1,),
        in_specs=in_specs,
        out_specs=pl.BlockSpec((BLOCK, D_MODEL), lambda i: (jnp.maximum(i - 1, 0), 0)),
        out_shape=jax.ShapeDtypeStruct((t, D_MODEL), F32),
        scratch_shapes=[
            pltpu.VMEM((BLOCK, MAIN_COLS), F32),
            pltpu.VMEM((BLOCK, LANES), F32),
            pltpu.VMEM((BLOCK, MIX_WIDTH), BF16),
            pltpu.VMEM((BLOCK + PAD_ROWS, CONV_CH), F32),
            pltpu.VMEM((GROUPS, STATE, GROUP_WIDTH), F32),
            pltpu.VMEM((BLOCK, CONV_CH), F32),
        ],
        compiler_params=pltpu.CompilerParams(
            dimension_semantics=("arbitrary",), vmem_limit_bytes=VMEM_LIMIT_BYTES),
        name="hybrid_layer",
    )(x2d, x2d, x2d, row(norm_mix_pre), w_main, w_dt, row(gm_ln_w), row(gm_ln_b), wcat, bs_e,
      conv_w.astype(F32), row(conv_b), dtb, a_pad, dskip_e, row(ssm_norm_w), mavg, tri,
      w_out.astype(BF16), row(norm_mix_post), row(norm_ffn_pre), w_up.astype(BF16),
      w_down.astype(BF16), row(norm_ffn_post))


def kernel(x, norm_mix_pre, w_in, gm_ln_w, gm_ln_b, gm_w_s, gm_b_s, conv_w, conv_b, dt_bias, a_log, d_skip, ssm_norm_w, w_out, norm_mix_post, norm_ffn_pre, w_up, w_down, norm_ffn_post):
    bsz, seq, d = x.shape
    depth = w_in.shape[0]
    x2d = x.reshape(bsz * seq, d)
    for i in range(depth):
        x2d = _layer(x2d, bsz, seq, norm_mix_pre[i], w_in[i], gm_ln_w[i], gm_ln_b[i], gm_w_s[i],
                     gm_b_s[i], conv_w[i], conv_b[i], dt_bias[i], a_log[i], d_skip[i],
                     ssm_norm_w[i], w_out[i], norm_mix_post[i], norm_ffn_pre[i], w_up[i],
                     w_down[i], norm_ffn_post[i])
    return x2d.reshape(bsz, seq, d)
```

```python
import functools
import math

import jax
import jax.numpy as jnp
from jax import lax
from jax.experimental import pallas as pl
from jax.experimental.pallas import tpu as pltpu

F32 = jnp.float32
BF16 = jnp.bfloat16

D_MODEL = 1024
GM_WIDTH = 512
HEAD_DIM = 64
HEADS = 8
CHUNK = 128
SSM_WIDTH = 512
GROUPS = 2
HEADS_PER_GROUP = HEADS // GROUPS
GROUP_WIDTH = HEADS_PER_GROUP * HEAD_DIM
STATE = 128
CONV = 4
CONV_CH = SSM_WIDTH + 2 * GROUPS * STATE
D_FF = 4 * D_MODEL
EPS = 1e-6
MAIN_COLS = 2 * GM_WIDTH + SSM_WIDTH + CONV_CH
Z_COL = 2 * GM_WIDTH
XBC_COL = Z_COL + SSM_WIDTH
MIX_WIDTH = GM_WIDTH + SSM_WIDTH
LANES = 128
MEAN_WIDTH = 256
PAD_ROWS = 8
LOG2E = 1.4426950408889634

VMEM_LIMIT_BYTES = 60 * 1024 * 1024

BLOCK = 512
PROJ_BLOCK = 512
FF_PIECE = 512


def _rms(x, w):
    return x * lax.rsqrt(jnp.mean(x * x, axis=-1, keepdims=True) + EPS) * w


def _split_dot_left(m, a, passes):
    pieces = []
    rem = a
    for _ in range(passes):
        piece = rem.astype(BF16)
        pieces.append(piece)
        rem = rem - piece.astype(F32)
    d = jnp.dot(m, jnp.concatenate(pieces, axis=1), preferred_element_type=F32)
    width = a.shape[1]
    acc = d[:, 0:width]
    for k in range(1, passes):
        acc = acc + d[:, k * width:(k + 1) * width]
    return acc


_GELU_A = -2.0 * math.sqrt(2.0 / math.pi) * LOG2E
_GELU_B = _GELU_A * 0.044715


def _gelu(x):
    return x / (1.0 + jnp.exp2(x * (_GELU_A + _GELU_B * (x * x))))


def _silu(x):
    return x / (1.0 + jnp.exp2(x * (-LOG2E)))


def _softplus(x):
    return jnp.maximum(x, 0.0) + jnp.log1p(jnp.exp(-jnp.abs(x)))


def _layer_kernel(
        xa_ref, xb_ref, xn_ref, nmix_ref, win_ref, wdt_ref, lnw_ref, lnb_ref, wcat_ref, bs_ref,
        convw_ref, convb_ref, dtb_ref, a_ref, dskip_ref, nrmw_ref, mavg_ref, tri_ref,
        wout_ref, npost_ref, npre_ref, wup_ref, wdown_ref, nffn_ref,
        out_ref,
        proj_ref, dtraw_ref, mix_ref, xpad_ref, state_ref, xc_ref,
        *, blocks_per_seq, n_blocks):
    i = pl.program_id(0)
    n_chunks = BLOCK // CHUNK
    first_of_seq = lax.rem(jnp.minimum(i, n_blocks - 1), blocks_per_seq) == 0

    def in_proj_pieces(xv, dst):
        hn = _rms(xv, nmix_ref[...]).astype(BF16)

        def main(n0):
            proj_ref[dst, n0:n0 + PROJ_BLOCK] = jnp.dot(
                hn, win_ref[:, n0:n0 + PROJ_BLOCK], preferred_element_type=F32)

        def dt():
            dtraw_ref[dst, :] = jnp.dot(hn, wdt_ref[...], preferred_element_type=F32)

        return [functools.partial(main, n0) for n0 in range(0, MAIN_COLS, PROJ_BLOCK)] + [dt]

    @pl.when(i == 0)
    def _():
        mix_ref[...] = jnp.zeros(mix_ref.shape, BF16)
        for piece in in_proj_pieces(xa_ref[0:BLOCK // 2, :], slice(0, BLOCK // 2)):
            piece()

    @pl.when(first_of_seq)
    def _():
        xpad_ref[0:PAD_ROWS, :] = jnp.zeros((PAD_ROWS, CONV_CH), F32)
        state_ref[...] = jnp.zeros(state_ref.shape, F32)

    lane = lax.broadcasted_iota(jnp.int32, (CHUNK, LANES), 1)
    row = lax.broadcasted_iota(jnp.int32, (CHUNK, LANES), 0)
    causal = row >= lane
    low_half = lane < HEAD_DIM

    def pair_rhs(tile):
        return jnp.concatenate([jnp.where(low_half, tile, 0.0).astype(BF16),
                                jnp.where(low_half, 0.0, tile).astype(BF16)], axis=0)

    def gmlp(r):
        rows = slice(r * CHUNK, (r + 1) * CHUNK)

        def head_mean(a):
            ab = a.astype(BF16)
            col_blocks = range(0, GM_WIDTH, MEAN_WIDTH)
            stacked = jnp.concatenate([ab[:, c0:c0 + MEAN_WIDTH] for c0 in col_blocks], axis=0)
            m = jnp.dot(stacked, mavg_ref[...], preferred_element_type=F32)
            return jnp.concatenate(
                [m[k * CHUNK:(k + 1) * CHUNK, :] for k in range(len(col_blocks))], axis=1)

        v = _gelu(proj_ref[rows, GM_WIDTH:2 * GM_WIDTH])
        mu = head_mean(v)
        yield
        d = v - mu
        var = head_mean(d * d)
        yield
        vn = d * lax.rsqrt(var + EPS) * lnw_ref[...] + lnb_ref[...]
        for k in range(GM_WIDTH // LANES):
            cols = slice(k * LANES, (k + 1) * LANES)
            mixed = jnp.dot(wcat_ref[k], pair_rhs(vn[:, cols]),
                            preferred_element_type=F32) + bs_ref[:, cols]
            mix_ref[rows, cols] = (_gelu(proj_ref[rows, cols]) * mixed).astype(BF16)
        yield

    def ssd(r):
        r0 = r * CHUNK
        rows = slice(r0, r0 + CHUNK)
        prows = slice(r0 + PAD_ROWS, r0 + PAD_ROWS + CHUNK)

        dt = _softplus(dtraw_ref[rows, :] + dtb_ref[...])
        dta = dt * (a_ref[...] * LOG2E)
        a_cs = _split_dot_left(tri_ref[...], dta, 3)
        yield

        xpad_ref[prows, :] = proj_ref[rows, XBC_COL:MAIN_COLS]
        for ct in range(CONV_CH // LANES):
            cols = slice(ct * LANES, (ct + 1) * LANES)
            acc = convb_ref[:, cols] + convw_ref[CONV - 1:CONV, cols] * xpad_ref[prows, cols]
            for k in range(1, CONV):
                acc = acc + convw_ref[CONV - 1 - k:CONV - k, cols] * xpad_ref[
                    r0 + PAD_ROWS - k:r0 + PAD_ROWS - k + CHUNK, cols]
            xc_ref[rows, cols] = _silu(acc)

        total = a_cs[CHUNK - 1:CHUNK, :]
        w_end = dt * jnp.exp2(total - a_cs)
        tr = jnp.where(lane < HEADS, a_cs - jnp.log(dt) * LOG2E, pltpu.roll(w_end, HEADS, 1)).T

        b_t, cb, y_off, state = [], [], [], []
        for g in range(GROUPS):
            b_g = xc_ref[rows, SSM_WIDTH + g * STATE:SSM_WIDTH + (g + 1) * STATE]
            c_g = xc_ref[rows, SSM_WIDTH + (GROUPS + g) * STATE:
                         SSM_WIDTH + (GROUPS + g + 1) * STATE].astype(BF16)
            cb.append(lax.dot_general(c_g, b_g.astype(BF16), (((1,), (1,)), ((), ())),
                                      preferred_element_type=F32))
            state.append(state_ref[g])
            y_off.append(jnp.dot(c_g, state[g].astype(BF16), preferred_element_type=F32))
            b_t.append(b_g.T)
        yield

        y_parts = []
        for g in range(GROUPS):
            y_tiles = []
            for p in range(HEADS_PER_GROUP // 2):
                pcols = slice(p * LANES, (p + 1) * LANES)
                scores, b_scaled, e_acs = [], [], []
                for q in range(2):
                    hd = g * HEADS_PER_GROUP + 2 * p + q
                    acs_b = jnp.broadcast_to(a_cs[:, hd:hd + 1], (CHUNK, LANES))
                    seg = acs_b - tr[hd:hd + 1, :]
                    scores.append(
                        (cb[g] * jnp.exp2(jnp.where(causal, seg, -jnp.inf))).astype(BF16))
                    b_scaled.append((b_t[g] * tr[HEADS + hd:HEADS + hd + 1, :]).astype(BF16))
                    e_acs.append(jnp.exp2(acs_b))
                tcol = g * (HEADS_PER_GROUP // 2) + p
                rhs = pair_rhs(xc_ref[rows, tcol * LANES:(tcol + 1) * LANES])
                lhs = jnp.concatenate([jnp.concatenate(scores, axis=1),
                                       jnp.concatenate(b_scaled, axis=1)], axis=0)
                res = jnp.dot(lhs, rhs, preferred_element_type=F32)
                decay = jnp.where(low_half, e_acs[0], e_acs[1])
                y_tiles.append(res[0:CHUNK, :] + y_off[g][:, pcols] * decay)
                state_ref[g, :, pcols] = (state[g][:, pcols] * decay[CHUNK - 1:CHUNK, :]
                                          + res[CHUNK:2 * CHUNK, :])
            y_parts.append(jnp.concatenate(y_tiles, axis=1))

        for g in range(GROUPS):
            gcols = slice(g * GROUP_WIDTH, (g + 1) * GROUP_WIDTH)
            zs = proj_ref[rows, Z_COL + g * GROUP_WIDTH:Z_COL + (g + 1) * GROUP_WIDTH]
            yg = (y_parts[g] + dskip_ref[:, gcols] * xc_ref[rows, gcols]) * _silu(zs)
            yn = yg * lax.rsqrt(jnp.mean(yg * yg, axis=-1, keepdims=True) + EPS)
            mix_ref[rows, GM_WIDTH + g * GROUP_WIDTH:GM_WIDTH + (g + 1) * GROUP_WIDTH] = (
                yn * nrmw_ref[:, gcols]).astype(BF16)
        yield

    def mixers(r, fill):
        chains = [gmlp(r), ssd(r)]
        while chains:
            for chain in list(chains):
                try:
                    next(chain)
                except StopIteration:
                    chains.remove(chain)
                    continue
                fill()

    half_rows = BLOCK // 2
    cons = [dict() for _ in range(2)]

    def out_proj_piece(hf):
        hrows = slice(hf * half_rows, (hf + 1) * half_rows)
        o = jnp.dot(mix_ref[hrows, :], wout_ref[...], preferred_element_type=F32)
        x1 = xb_ref[hrows, :] + _rms(o, npost_ref[...])
        cons[hf]["x1"] = x1
        cons[hf]["h2"] = _rms(x1, npre_ref[...]).astype(BF16)

    ff_pieces = list(range(0, D_FF, FF_PIECE))
    full = {}

    def up_piece(f0):
        h2 = jnp.concatenate([cons[0]["h2"], cons[1]["h2"]], axis=0)
        up = jnp.dot(h2, wup_ref[:, f0:f0 + FF_PIECE], preferred_element_type=F32)
        full[f0] = jnp.square(jnp.maximum(up, 0.0)).astype(BF16)

    def down_piece(f0):
        dn = jnp.dot(full[f0], wdown_ref[f0:f0 + FF_PIECE, :], preferred_element_type=F32)
        full["acc"] = dn if "acc" not in full else full["acc"] + dn
        if f0 == ff_pieces[-1]:
            for hf in range(2):
                hrows = slice(hf * half_rows, (hf + 1) * half_rows)
                out_ref[hrows, :] = cons[hf]["x1"] + _rms(full["acc"][hrows, :], nffn_ref[...])

    out_proj_piece(0)
    out_proj_piece(1)
    queue = []
    mid = len(ff_pieces) // 2
    for part in (ff_pieces[:mid], ff_pieces[mid:]):
        queue += [functools.partial(up_piece, f0) for f0 in part]
        queue += [functools.partial(down_piece, f0) for f0 in part]
    per_slot = [4, 4, 4, 4]
    assert sum(per_slot) == len(queue)

    upper = slice(half_rows, BLOCK)
    proj_queue = (in_proj_pieces(xa_ref[upper, :], upper)
                  + in_proj_pieces(xn_ref[...], slice(0, half_rows)))
    proj_per_slot = len(proj_queue) // n_chunks

    for r in range(n_chunks):
        mine = [queue.pop(0) for _ in range(per_slot[r])]
        nxt = [proj_queue.pop(0) for _ in range(proj_per_slot)]
        fillers = []
        while mine or nxt:
            if mine:
                fillers.append(mine.pop(0))
            if nxt:
                fillers.append(nxt.pop(0))

        def fill(fillers=fillers):
            if fillers:
                fillers.pop(0)()

        mixers(r, fill)
        while fillers:
            fill()

    xpad_ref[0:PAD_ROWS, :] = xpad_ref[BLOCK:BLOCK + PAD_ROWS, :]


def _layer(x2d, bsz, seq, norm_mix_pre, w_in, gm_ln_w, gm_ln_b, gm_w_s, gm_b_s, conv_w, conv_b,
           dt_bias, a_log, d_skip, ssm_norm_w, w_out, norm_mix_post, norm_ffn_pre, w_up, w_down,
           norm_ffn_post):
    t = bsz * seq
    assert seq % BLOCK == 0, (seq, BLOCK)
    n_blocks = t // BLOCK
    row = lambda p: p.reshape(1, -1).astype(F32)

    w_main = w_in[:, :MAIN_COLS].astype(BF16)
    w_dt = jnp.pad(w_in[:, MAIN_COLS:], ((0, 0), (0, LANES - HEADS))).astype(BF16)
    causal = jnp.tril(jnp.ones((CHUNK, CHUNK), dtype=bool))
    w_s = jnp.where(causal[None], gm_w_s, 0.0).astype(BF16)
    wcat = w_s.reshape(HEADS // 2, 2, CHUNK, CHUNK).transpose(0, 2, 1, 3).reshape(
        HEADS // 2, CHUNK, 2 * CHUNK)
    bs_e = jnp.repeat(gm_b_s.T.astype(F32), HEAD_DIM, axis=1)
    dtb = jnp.pad(dt_bias.astype(F32), (0, LANES - HEADS)).reshape(1, LANES)
    a_pad = jnp.pad(-jnp.exp(a_log.astype(F32)), (0, LANES - HEADS)).reshape(1, LANES)
    dskip_e = jnp.repeat(d_skip.astype(F32), HEAD_DIM).reshape(1, SSM_WIDTH)
    head_of_col = jnp.arange(MEAN_WIDTH) // HEAD_DIM
    mavg = jnp.where(head_of_col[:, None] == head_of_col[None, :], 1.0 / HEAD_DIM, 0.0).astype(BF16)
    tri = causal.astype(BF16)

    def const(shape):
        zeros = (0,) * len(shape)
        return pl.BlockSpec(shape, lambda i: zeros, pipeline_mode=pl.Buffered(1))

    last = n_blocks - 1
    last_half = 2 * n_blocks - 1
    in_specs = [
        pl.BlockSpec((BLOCK, D_MODEL), lambda i: (jnp.minimum(i, last), 0)),
        pl.BlockSpec((BLOCK, D_MODEL), lambda i: (jnp.maximum(i - 1, 0), 0)),
        pl.BlockSpec((BLOCK // 2, D_MODEL),
                     lambda i: (jnp.minimum(2 * (i + 1), last_half), 0)),
        const((1, D_MODEL)),
        const((D_MODEL, MAIN_COLS)),
        const((D_MODEL, LANES)),
        const((1, GM_WIDTH)),
        const((1, GM_WIDTH)),
        const((HEADS // 2, CHUNK, 2 * CHUNK)),
        const((CHUNK, GM_WIDTH)),
        const((CONV, CONV_CH)),
        const((1, CONV_CH)),
        const((1, LANES)),
        const((1, LANES)),
        const((1, SSM_WIDTH)),
        const((1, SSM_WIDTH)),
        const((MEAN_WIDTH, MEAN_WIDTH)),
        const((CHUNK, CHUNK)),
        const((MIX_WIDTH, D_MODEL)),
        const((1, D_MODEL)),
        const((1, D_MODEL)),
        const((D_MODEL, D_FF)),
        const((D_FF, D_MODEL)),
        const((1, D_MODEL)),
    ]
    return pl.pallas_call(
        functools.partial(_layer_kernel, blocks_per_seq=seq // BLOCK, n_blocks=n_blocks),
        grid=(n_blocks + 1,),
        in_specs=in_specs,
        out_specs=pl.BlockSpec((BLOCK, D_MODEL), lambda i: (jnp.maximum(i - 1, 0), 0)),
        out_shape=jax.ShapeDtypeStruct((t, D_MODEL), F32),
        scratch_shapes=[
            pltpu.VMEM((BLOCK, MAIN_COLS), F32),
            pltpu.VMEM((BLOCK, LANES), F32),
            pltpu.VMEM((BLOCK, MIX_WIDTH), BF16),
            pltpu.VMEM((BLOCK + PAD_ROWS, CONV_CH), F32),
            pltpu.VMEM((GROUPS, STATE, GROUP_WIDTH), F32),
            pltpu.VMEM((BLOCK, CONV_CH), F32),
        ],
        compiler_params=pltpu.CompilerParams(
            dimension_semantics=("arbitrary",), vmem_limit_bytes=VMEM_LIMIT_BYTES),
        name="hybrid_layer",
    )(x2d, x2d, x2d, row(norm_mix_pre), w_main, w_dt, row(gm_ln_w), row(gm_ln_b), wcat, bs_e,
      conv_w.astype(F32), row(conv_b), dtb, a_pad, dskip_e, row(ssm_norm_w), mavg, tri,
      w_out.astype(BF16), row(norm_mix_post), row(norm_ffn_pre), w_up.astype(BF16),
      w_down.astype(BF16), row(norm_ffn_post))


def kernel(x, norm_mix_pre, w_in, gm_ln_w, gm_ln_b, gm_w_s, gm_b_s, conv_w, conv_b, dt_bias, a_log, d_skip, ssm_norm_w, w_out, norm_mix_post, norm_ffn_pre, w_up, w_down, norm_ffn_post):
    bsz, seq, d = x.shape
    depth = w_in.shape[0]
    x2d = x.reshape(bsz * seq, d)
    for i in range(depth):
        x2d = _layer(x2d, bsz, seq, norm_mix_pre[i], w_in[i], gm_ln_w[i], gm_ln_b[i], gm_w_s[i],
                     gm_b_s[i], conv_w[i], conv_b[i], dt_bias[i], a_log[i], d_skip[i],
                     ssm_norm_w[i], w_out[i], norm_mix_post[i], norm_ffn_pre[i], w_up[i],
                     w_down[i], norm_ffn_post[i])
    return x2d.reshape(bsz, seq, d)
```
